```python
import math
import jax, jax.numpy as jnp
from jax import lax
import numpy as np

D_MODEL = 1024
BATCH = 8
SEQ = 2048
DEPTH = 4
DEC_BATCH = 128
DEC_SEQ = 8
PAST_LEN = 16384
PAGE_SIZE = 128

N_MIXERS = 2
N_HGRN = (DEPTH + 1) // 2
N_S5 = DEPTH // 2
HG_HEAD_DIM = 128
HG_HEADS = D_MODEL // HG_HEAD_DIM
HG_QK_SCALE = HG_HEAD_DIM ** -0.5
CHUNK = 16
S5_GROUP = 16
S5_GROUPS = D_MODEL // S5_GROUP
S5_STATE = 64
D_FF = -(-8 * D_MODEL // (3 * 256)) * 256
PLE_DIM = 256
ALPHA = (2 * DEPTH) ** 0.25
BETA = (8 * DEPTH) ** -0.25
LN_EPS = 1e-5
RMS_EPS = 1e-6

kernel_name = 'hgrn2_s5_hybrid_decoder_step'


def layer_norm(x, w, b):
    xf = x.astype(jnp.float32)
    mu = jnp.mean(xf, axis=-1, keepdims=True)
    var = jnp.mean(jnp.square(xf - mu), axis=-1, keepdims=True)
    y = (xf - mu) * lax.rsqrt(var + LN_EPS) * w.astype(jnp.float32) + b.astype(jnp.float32)
    return y.astype(x.dtype)


def rms_norm(x, w):
    xf = x.astype(jnp.float32)
    y = xf * lax.rsqrt(jnp.mean(jnp.square(xf), axis=-1, keepdims=True) + RMS_EPS) * w.astype(jnp.float32)
    return y.astype(x.dtype)


def hgrn2_chunk_scan(q, k, v, logf, s0):
    bsz, t = q.shape[0], q.shape[1]
    pad = (-t) % CHUNK
    pw = ((0, 0), (0, pad), (0, 0), (0, 0))
    q, k, v, logf = (jnp.pad(a, pw) for a in (q, k, v, logf))
    nc = (t + pad) // CHUNK

    def to_chunks(a):
        return a.reshape(bsz, nc, CHUNK, a.shape[2], a.shape[3]).swapaxes(0, 1)

    causal = jnp.tril(jnp.ones((CHUNK, CHUNK), dtype=bool))

    def step(s, xs):
        qc, kc, vc, lc = xs
        b = jnp.cumsum(lc, axis=1)
        bm = b[:, CHUNK // 2 - 1][:, None]
        bl = b[:, -1]
        qh = qc * jnp.exp(b - bm)
        kh = kc * jnp.exp(bm - b)
        att = jnp.where(causal, jnp.einsum('bthk,bshk->bhts', qh, kh), 0.0)
        o = (jnp.einsum('bhts,bshv->bthv', att, vc)
             + jnp.einsum('bthk,bhkv->bthv', qc * jnp.exp(b), s))
        s_new = (jnp.exp(bl)[..., None] * s
                 + jnp.einsum('bshk,bshv->bhkv', kc * jnp.exp(bl[:, None] - b), vc))
        return s_new, o

    s_fin, o = lax.scan(step, s0, (to_chunks(q), to_chunks(k), to_chunks(v), to_chunks(logf)))
    o = o.swapaxes(0, 1).reshape(bsz, nc * CHUNK, o.shape[3], o.shape[4])[:, :t]
    return o, s_fin


def hgrn2_mixer(x, s0, w_in, lb, gnorm_w, w_out):
    bsz, t, _ = x.shape
    q, fz, v, g = jnp.split(x @ w_in, 4, axis=-1)

    def heads(a):
        return a.reshape(bsz, t, HG_HEADS, HG_HEAD_DIM)

    lbf = lb.astype(jnp.float32)
    f = lbf + (1.0 - lbf) * jax.nn.sigmoid(fz.astype(jnp.float32))
    qh = heads(jax.nn.silu(q.astype(jnp.float32)) * HG_QK_SCALE)
    kh = heads(1.0 - f)
    o, s = hgrn2_chunk_scan(qh, kh, heads(v.astype(jnp.float32)), heads(jnp.log(f)),
                            s0.astype(jnp.float32))
    o = rms_norm(o, gnorm_w) * jax.nn.silu(heads(g.astype(jnp.float32)))
    o = o.reshape(bsz, t, D_MODEL).astype(x.dtype)
    return o @ w_out, s


def s5_combine(e1, e2):
    a1, b1 = e1
    a2, b2 = e2
    return a1 * a2, a2 * b1 + b2


def s5_mixer(x, h0_re, h0_im, a_re, a_im, b_re, b_im, c_re, c_im, d_skip, log_step, w_glu):
    bsz, t, _ = x.shape
    f32 = jnp.float32
    u = x.astype(f32).reshape(bsz, t, S5_GROUPS, S5_GROUP)
    lam = lax.complex(a_re.astype(f32), a_im.astype(f32))
    delta = jnp.exp(log_step.astype(f32))[:, None]
    lam_bar = jnp.exp(lam * delta)
    b_bar = ((lam_bar - 1.0) / lam)[..., None] * lax.complex(b_re.astype(f32), b_im.astype(f32))
    c_mat = lax.complex(c_re.astype(f32), c_im.astype(f32))
    bu = jnp.einsum('btgc,gpc->btgp', u.astype(jnp.complex64), b_bar)
    h0 = lax.complex(h0_re.astype(f32), h0_im.astype(f32))
    bu = bu.at[:, 0].add(lam_bar * h0)
    a_el = jnp.broadcast_to(lam_bar, (1, t, S5_GROUPS, S5_STATE))
    _, h = lax.associative_scan(s5_combine, (a_el, bu), axis=1)
    y = jnp.einsum('btgp,gcp->btgc', h, c_mat).real + d_skip.astype(f32).reshape(S5_GROUPS, S5_GROUP) * u
    y = jax.nn.gelu(y.reshape(bsz, t, D_MODEL)).astype(x.dtype)
    val, gate = jnp.split(y @ w_glu, 2, axis=-1)
    h_last = h[:, -1]
    return val * jax.nn.sigmoid(gate), h_last.real, h_last.imag


def swiglu(x, w_gate_up, w_down):
    g, u = jnp.split(x @ w_gate_up, 2, axis=-1)
    return (jax.nn.silu(g) * u) @ w_down


def per_layer_embed(x, p_i, w_proj, w_gate, norm_w):
    e = (p_i @ w_proj).astype(jnp.float32) * jax.nn.sigmoid((x @ w_gate).astype(jnp.float32))
    return x + rms_norm(e, norm_w).astype(x.dtype)


def trunk(x, p, st_hg, st_re, st_im, lb,
          hg_w_in, hg_gnorm_w, hg_w_out,
          s5_a_re, s5_a_im, s5_b_re, s5_b_im, s5_c_re, s5_c_im, s5_d, s5_log_step, s5_w_glu,
          ln_mix_w, ln_mix_b, ffn_w_gate_up, ffn_w_down, ln_ffn_w, ln_ffn_b,
          ple_w_proj, ple_w_gate, ple_norm_w):
    new_hg, new_re, new_im = [], [], []
    for i in range(DEPTH):
        j = i // N_MIXERS
        if i % N_MIXERS == 0:
            mix, s = hgrn2_mixer(x, st_hg[j], hg_w_in[j], lb[j], hg_gnorm_w[j], hg_w_out[j])
            new_hg.append(s)
        else:
            mix, hr, hi = s5_mixer(x, st_re[j], st_im[j], s5_a_re[j], s5_a_im[j], s5_b_re[j], s5_b_im[j],
                                   s5_c_re[j], s5_c_im[j], s5_d[j], s5_log_step[j], s5_w_glu[j])
            new_re.append(hr)
            new_im.append(hi)
        x = layer_norm(ALPHA * x + mix, ln_mix_w[i], ln_mix_b[i])
        x = layer_norm(ALPHA * x + swiglu(x, ffn_w_gate_up[i], ffn_w_down[i]), ln_ffn_w[i], ln_ffn_b[i])
        x = per_layer_embed(x, p[i], ple_w_proj[i], ple_w_gate[i], ple_norm_w[i])
    return x, jnp.stack(new_hg), jnp.stack(new_re), jnp.stack(new_im)


def setup_inputs(seed: int = 0) -> dict:
    key = jax.random.key(seed)
    ks = jax.random.split(key, 32)
    f32 = jnp.float32

    def nrm(k, shape, s):
        return jax.random.normal(k, shape, f32) * s

    s5_a_im = (jnp.pi * jnp.arange(S5_STATE, dtype=f32))[None, None, :] + nrm(ks[12], (N_S5, S5_GROUPS, S5_STATE), 0.01)
    return {
        'x_prompt': nrm(ks[0], (BATCH, SEQ, D_MODEL), 1.0),
        'x_sample': nrm(ks[1], (DEC_BATCH, DEC_SEQ, D_MODEL), 1.0),
        'state_hgrn': nrm(ks[2], (N_HGRN, DEC_BATCH, HG_HEADS, HG_HEAD_DIM, HG_HEAD_DIM), 0.5),
        'state_s5_re': nrm(ks[3], (N_S5, DEC_BATCH, S5_GROUPS, S5_STATE), 0.1),
        'state_s5_im': nrm(ks[4], (N_S5, DEC_BATCH, S5_GROUPS, S5_STATE), 0.1),
        'p_prompt': nrm(ks[5], (DEPTH, BATCH, SEQ, PLE_DIM), 1.0),
        'p_sample': nrm(ks[6], (DEPTH, DEC_BATCH, DEC_SEQ, PLE_DIM), 1.0),
        'hg_w_in': nrm(ks[7], (N_HGRN, D_MODEL, 4 * D_MODEL), D_MODEL ** -0.5),
        'hg_lower_bounds': nrm(ks[8], (N_HGRN, D_MODEL), 0.1),
        'hg_gnorm_w': 1.0 + nrm(ks[9], (N_HGRN, HG_HEAD_DIM), 0.01),
        'hg_w_out': nrm(ks[10], (N_HGRN, D_MODEL, D_MODEL), BETA * D_MODEL ** -0.5),
        's5_a_re': -0.5 + nrm(ks[11], (N_S5, S5_GROUPS, S5_STATE), 0.01),
        's5_a_im': s5_a_im,
        's5_b_re': nrm(ks[13], (N_S5, S5_GROUPS, S5_STATE, S5_GROUP), (2 * S5_GROUP) ** -0.5),
        's5_b_im': nrm(ks[14], (N_S5, S5_GROUPS, S5_STATE, S5_GROUP), (2 * S5_GROUP) ** -0.5),
        's5_c_re': nrm(ks[15], (N_S5, S5_GROUPS, S5_GROUP, S5_STATE), (2 * S5_STATE) ** -0.5),
        's5_c_im': nrm(ks[16], (N_S5, S5_GROUPS, S5_GROUP, S5_STATE), (2 * S5_STATE) ** -0.5),
        's5_d': nrm(ks[17], (N_S5, D_MODEL), 1.0),
        's5_log_step': jax.random.uniform(ks[18], (N_S5, S5_GROUPS), f32, math.log(1e-3), math.log(1e-1)),
        's5_w_glu': nrm(ks[19], (N_S5, D_MODEL, 2 * D_MODEL), BETA * D_MODEL ** -0.5),
        'ln_mix_w': 1.0 + nrm(ks[20], (DEPTH, D_MODEL), 0.01),
        'ln_mix_b': nrm(ks[21], (DEPTH, D_MODEL), 0.01),
        'ffn_w_gate_up': nrm(ks[22], (DEPTH, D_MODEL, 2 * D_FF), D_MODEL ** -0.5),
        'ffn_w_down': nrm(ks[23], (DEPTH, D_FF, D_MODEL), BETA * D_FF ** -0.5),
        'ln_ffn_w': 1.0 + nrm(ks[24], (DEPTH, D_MODEL), 0.01),
        'ln_ffn_b': nrm(ks[25], (DEPTH, D_MODEL), 0.01),
        'ple_w_proj': nrm(ks[26], (DEPTH, PLE_DIM, D_MODEL), PLE_DIM ** -0.5),
        'ple_w_gate': nrm(ks[27], (DEPTH, D_MODEL, D_MODEL), D_MODEL ** -0.5),
        'ple_norm_w': 1.0 + nrm(ks[28], (DEPTH, D_MODEL), 0.01),
    }


def reference(x_prompt, x_sample, state_hgrn, state_s5_re, state_s5_im, p_prompt, p_sample,
              hg_w_in, hg_lower_bounds, hg_gnorm_w, hg_w_out,
              s5_a_re, s5_a_im, s5_b_re, s5_b_im, s5_c_re, s5_c_im, s5_d, s5_log_step, s5_w_glu,
              ln_mix_w, ln_mix_b, ffn_w_gate_up, ffn_w_down, ln_ffn_w, ln_ffn_b,
              ple_w_proj, ple_w_gate, ple_norm_w):
    lb_soft = jax.nn.softmax(hg_lower_bounds.astype(jnp.float32), axis=0)
    lb = jnp.cumsum(lb_soft, axis=0) - lb_soft[0]
    weights = (hg_w_in, hg_gnorm_w, hg_w_out,
               s5_a_re, s5_a_im, s5_b_re, s5_b_im, s5_c_re, s5_c_im, s5_d, s5_log_step, s5_w_glu,
               ln_mix_w, ln_mix_b, ffn_w_gate_up, ffn_w_down, ln_ffn_w, ln_ffn_b,
               ple_w_proj, ple_w_gate, ple_norm_w)
    zero_hg = jnp.zeros((N_HGRN, BATCH, HG_HEADS, HG_HEAD_DIM, HG_HEAD_DIM), jnp.float32)
    zero_s5 = jnp.zeros((N_S5, BATCH, S5_GROUPS, S5_STATE), jnp.float32)
    y_prompt, hg_p, re_p, im_p = trunk(x_prompt, p_prompt, zero_hg, zero_s5, zero_s5, lb, *weights)
    y_sample, hg_s, re_s, im_s = trunk(x_sample, p_sample, state_hgrn, state_s5_re, state_s5_im, lb, *weights)
    return (y_prompt, y_sample, hg_p, re_p, im_p, hg_s, re_s, im_s)
```

```python
import functools
import math

import jax
import jax.numpy as jnp
from jax import lax
from jax.experimental import pallas as pl
from jax.experimental.pallas import tpu as pltpu

F32 = jnp.float32
BF16 = jnp.bfloat16

D_MODEL = 1024
HG_HEAD_DIM = 128
HG_HEADS = D_MODEL // HG_HEAD_DIM
HG_QK_SCALE = HG_HEAD_DIM ** -0.5
HG_CHUNK = 16
S5_GROUP = 16
S5_GROUPS = D_MODEL // S5_GROUP
S5_STATE = 64
S5_COLS = S5_GROUPS * S5_STATE
DEPTH = 4
ALPHA = (2 * DEPTH) ** 0.25
LN_EPS = 1e-5
RMS_EPS = 1e-6

V7X_VMEM_BYTES = 64 * 1024 * 1024
VMEM_LIMIT_BYTES = V7X_VMEM_BYTES - 8 * 1024 * 1024
MXU_TILE = 256
VREG_ELEMS = 8 * 128

NT_DIMS = (((1,), (1,)), ((), ()))
TN_DIMS = (((0,), (0,)), ((), ()))


def _dot(a, b):
    return jnp.dot(a, b, preferred_element_type=F32)


def _sigmoid(x):
    return 1.0 / (1.0 + jnp.exp(-x))


def _layer_norm(x, w, b):
    mu = jnp.mean(x, axis=-1, keepdims=True)
    xc = x - mu
    var = jnp.mean(xc * xc, axis=-1, keepdims=True)
    return xc * lax.rsqrt(var + LN_EPS) * w + b


def _rms_norm(x, w):
    return x * lax.rsqrt(jnp.mean(x * x, axis=-1, keepdims=True) + RMS_EPS) * w


def _resident(shape):
    nd = len(shape)
    return pl.BlockSpec(shape, lambda *_: (0,) * nd, pipeline_mode=pl.Buffered(1))


def _dense_kernel(x_ref, p_ref, wgu_ref, wd_ref, lnw_ref, lnb_ref, wproj_ref, wgate_ref, nw_ref, o_ref,
                  *, d_ff, ff_chunk):
    x = x_ref[...]
    xb = x.astype(BF16)
    acc = jnp.zeros(x.shape, F32)
    for j in range(d_ff // ff_chunk):
        lo = j * ff_chunk
        g = _dot(xb, wgu_ref[:, lo:lo + ff_chunk])
        u = _dot(xb, wgu_ref[:, d_ff + lo:d_ff + lo + ff_chunk])
        h = (g * _sigmoid(g)) * u
        acc = acc + _dot(h.astype(BF16), wd_ref[lo:lo + ff_chunk, :])
    x2 = _layer_norm(ALPHA * x + acc, lnw_ref[...], lnb_ref[...])
    e = _dot(p_ref[...].astype(BF16), wproj_ref[...]) * _sigmoid(_dot(x2.astype(BF16), wgate_ref[...]))
    o_ref[...] = x2 + _rms_norm(e, nw_ref[...])


def _dense_call(x, p, wgu, wd, lnw, lnb, wproj, wgate, nw, *, tm):
    n, d = x.shape
    d_ff = wd.shape[0]
    ple = p.shape[1]
    row = lambda i: (i, 0)
    kern = functools.partial(_dense_kernel, d_ff=d_ff, ff_chunk=MXU_TILE)
    return pl.pallas_call(
        kern,
        grid=(n // tm,),
        in_specs=[pl.BlockSpec((tm, d), row), pl.BlockSpec((tm, ple), row),
                  _resident(wgu.shape), _resident(wd.shape), _resident(lnw.shape), _resident(lnb.shape),
                  _resident(wproj.shape), _resident(wgate.shape), _resident(nw.shape)],
        out_specs=pl.BlockSpec((tm, d), row),
        out_shape=jax.ShapeDtypeStruct((n, d), F32),
        compiler_params=pltpu.CompilerParams(dimension_semantics=("parallel",),
                                             vmem_limit_bytes=VMEM_LIMIT_BYTES),
        name="dense_ffn_ple",
    )(x, p, wgu, wd, lnw, lnb, wproj, wgate, nw)


def _s5_kernel(*refs, bt, tc, has_state):
    if has_state:
        x_ref, h0r_ref, h0i_ref = refs[:3]
        refs = refs[3:]
    else:
        x_ref = refs[0]
        refs = refs[1:]
    (lamr_ref, lami_ref, bdr_ref, bdi_ref, cd_ref, dsk_ref, wglu_ref, lnw_ref, lnb_ref,
     o_ref, hro_ref, hio_ref, bur_ref, bui_ref, hsr_ref, hsi_ref) = refs
    d = D_MODEL
    ti = pl.program_id(1)

    @pl.when(ti == 0)
    def _():
        if has_state:
            hsr_ref[...] = h0r_ref[...]
            hsi_ref[...] = h0i_ref[...]
        else:
            hsr_ref[...] = jnp.zeros(hsr_ref.shape, F32)
            hsi_ref[...] = jnp.zeros(hsi_ref.shape, F32)

    u = jnp.concatenate([x_ref[:, t * d:(t + 1) * d] for t in range(tc)], axis=0)
    ub = u.astype(BF16)
    n_kt = d // MXU_TILE
    kt_cols = S5_COLS // n_kt
    for kt in range(n_kt):
        uk = ub[:, kt * MXU_TILE:(kt + 1) * MXU_TILE]
        bur_ref[:, kt * kt_cols:(kt + 1) * kt_cols] = _dot(uk, bdr_ref[kt])
        bui_ref[:, kt * kt_cols:(kt + 1) * kt_cols] = _dot(uk, bdi_ref[kt])

    slab = 4 * VREG_ELEMS // bt
    for s in range(S5_COLS // slab):
        cs = slice(s * slab, (s + 1) * slab)
        lr = jnp.broadcast_to(lamr_ref[:, cs], (bt, slab))
        li = jnp.broadcast_to(lami_ref[:, cs], (bt, slab))
        hr = hsr_ref[:, cs]
        hi = hsi_ref[:, cs]
        for t in range(tc):
            rs = slice(t * bt, (t + 1) * bt)
            nr = lr * hr - li * hi + bur_ref[rs, cs]
            ni = lr * hi + li * hr + bui_ref[rs, cs]
            bur_ref[rs, cs] = nr
            bui_ref[rs, cs] = ni
            hr, hi = nr, ni
        hsr_ref[:, cs] = hr
        hsi_ref[:, cs] = hi
    hro_ref[...] = hsr_ref[...]
    hio_ref[...] = hsi_ref[...]

    ys = []
    for nt in range(n_kt):
        hrb = bur_ref[:, nt * kt_cols:(nt + 1) * kt_cols].astype(BF16)
        hib = bui_ref[:, nt * kt_cols:(nt + 1) * kt_cols].astype(BF16)
        ys.append(_dot(hrb, cd_ref[nt, :kt_cols, :]) + _dot(hib, cd_ref[nt, kt_cols:, :]))
    y = jnp.concatenate(ys, axis=1) + dsk_ref[...] * u
    y = y * (0.5 * (1.0 + jnp.tanh(math.sqrt(2.0 / math.pi) * (y + 0.044715 * (y * y * y)))))
    z = _dot(y.astype(BF16), wglu_ref[...])
    mix = z[:, :d] * _sigmoid(z[:, d:])
    xn = _layer_norm(ALPHA * u + mix, lnw_ref[...], lnb_ref[...])
    for t in range(tc):
        o_ref[:, t * d:(t + 1) * d] = xn[t * bt:(t + 1) * bt, :]


def _s5_call(x2d, h0r, h0i, lamr, lami, bdr, bdi, cd, dsk, wglu, lnw, lnb, *, bt, tc):
    bsz, td = x2d.shape
    d = D_MODEL
    t_len = td // d
    has_state = h0r is not None
    xspec = pl.BlockSpec((bt, tc * d), lambda b, t: (b, t))
    sspec = pl.BlockSpec((bt, S5_COLS), lambda b, t: (b, 0))
    in_specs = [xspec] + ([sspec, sspec] if has_state else []) + [
        _resident(lamr.shape), _resident(lami.shape), _resident(bdr.shape), _resident(bdi.shape),
        _resident(cd.shape), _resident(dsk.shape), _resident(wglu.shape), _resident(lnw.shape),
        _resident(lnb.shape)]
    args = [x2d] + ([h0r, h0i] if has_state else []) + [lamr, lami, bdr, bdi, cd, dsk, wglu, lnw, lnb]
    rows = bt * tc
    kern = functools.partial(_s5_kernel, bt=bt, tc=tc, has_state=has_state)
    return pl.pallas_call(
        kern,
        grid=(bsz // bt, t_len // tc),
        in_specs=in_specs,
        out_specs=[xspec, sspec, sspec],
        out_shape=[jax.ShapeDtypeStruct((bsz, td), F32),
                   jax.ShapeDtypeStruct((bsz, S5_COLS), F32),
                   jax.ShapeDtypeStruct((bsz, S5_COLS), F32)],
        scratch_shapes=[pltpu.VMEM((rows, S5_COLS), F32), pltpu.VMEM((rows, S5_COLS), F32),
                        pltpu.VMEM((bt, S5_COLS), F32), pltpu.VMEM((bt, S5_COLS), F32)],
        compiler_params=pltpu.CompilerParams(dimension_semantics=("parallel", "arbitrary"),
                                             vmem_limit_bytes=VMEM_LIMIT_BYTES),
        name="s5_mixer",
    )(*args)


def _s5_weights(a_re, a_im, b_re, b_im, c_re, c_im, log_step):
    delta = jnp.exp(log_step)[:, None]
    er = jnp.exp(a_re * delta)
    lr = er * jnp.cos(a_im * delta)
    li = er * jnp.sin(a_im * delta)
    den = a_re * a_re + a_im * a_im
    qr = ((lr - 1.0) * a_re + li * a_im) / den
    qi = (li * a_re - (lr - 1.0) * a_im) / den
    bbr = qr[..., None] * b_re - qi[..., None] * b_im
    bbi = qr[..., None] * b_im + qi[..., None] * b_re
    g_per = MXU_TILE // S5_GROUP
    n_kt = S5_GROUPS // g_per
    eye = jnp.eye(g_per, dtype=F32)

    def in_proj(bb):
        w = bb.reshape(n_kt, g_per, S5_STATE, S5_GROUP).transpose(0, 1, 3, 2)
        w = w[:, :, :, None, :] * eye[None, :, None, :, None]
        return w.reshape(n_kt, g_per * S5_GROUP, g_per * S5_STATE).astype(BF16)

    def out_proj(cc):
        w = cc.reshape(n_kt, g_per, S5_GROUP, S5_STATE).transpose(0, 1, 3, 2)
        w = w[:, :, :, None, :] * eye[None, :, None, :, None]
        return w.reshape(n_kt, g_per * S5_STATE, g_per * S5_GROUP)

    cd = jnp.concatenate([out_proj(c_re), -out_proj(c_im)], axis=1).astype(BF16)
    return (lr.reshape(1, S5_COLS), li.reshape(1, S5_COLS), in_proj(bbr), in_proj(bbi), cd)


def _hgrn_kernel(*refs, rows, c, seq):
    if seq:
        (x_ref, win_ref, lb_ref, gn_ref, wout_ref, lnw_ref, lnb_ref,
         o_ref, so_ref, st_ref, qd_ref, kd_ref, v_ref, e_ref, oi_ref) = refs
        s0_ref = None
    else:
        (x_ref, s0_ref, win_ref, lb_ref, gn_ref, wout_ref, lnw_ref, lnb_ref,
         o_ref, so_ref, qd_ref, kd_ref, v_ref, e_ref, oi_ref) = refs
        st_ref = None
    d = D_MODEL
    hd = HG_HEAD_DIM
    nsub = rows // c
    lg = c.bit_length() - 1

    if seq:
        ti = pl.program_id(1)

        @pl.when(ti == 0)
        def _():
            st_ref[...] = jnp.zeros(st_ref.shape, F32)

    x = x_ref[...]
    proj = _dot(x.astype(BF16), win_ref[...])
    q = proj[:, :d]
    fz = proj[:, d:2 * d]
    v = proj[:, 2 * d:3 * d]
    g = proj[:, 3 * d:]
    lb = lb_ref[...]
    f = lb + (1.0 - lb) * _sigmoid(fz)
    logf = jnp.log(f)
    kk = 1.0 - f
    qs = q * _sigmoid(q) * HG_QK_SCALE

    ri = lax.broadcasted_iota(jnp.int32, (rows, rows), 0)
    ci = lax.broadcasted_iota(jnp.int32, (rows, rows), 1)
    causal = jnp.logical_and((ri >> lg) == (ci >> lg), ci <= ri)
    tri = jnp.where(causal, 1.0, 0.0).astype(BF16)
    l_hi = logf.astype(BF16)
    l_lo = (logf - l_hi.astype(F32)).astype(BF16)
    b = _dot(tri, l_hi) + _dot(tri, l_lo)
    b3 = b.reshape(nsub, c, d)
    bm = b3[:, c // 2 - 1:c // 2, :]
    bl = b3[:, c - 1:c, :]
    d1 = (b3 - bm).reshape(rows, d)
    qh = (qs * jnp.exp(d1)).astype(BF16)
    kh = (kk * jnp.exp(-d1)).astype(BF16)
    qd_ref[...] = qs * jnp.exp(b)
    kd_ref[...] = kk * jnp.exp((bl - b3).reshape(rows, d))
    v_ref[...] = v
    e_ref[...] = jnp.exp(bl)
    vb = v.astype(BF16)

    for h in range(HG_HEADS):
        cs = slice(h * hd, (h + 1) * hd)
        att = lax.dot_general(qh[:, cs], kh[:, cs], NT_DIMS, preferred_element_type=F32)
        att = jnp.where(causal, att, 0.0)
        oi_ref[:, cs] = _dot(att.astype(BF16), vb[:, cs])

    def body(j, carry):
        r0 = pl.multiple_of(j * c, c)
        for h in range(HG_HEADS):
            cs = slice(h * hd, (h + 1) * hd)
            qd = qd_ref[pl.ds(r0, c), cs].astype(BF16)
            kd = kd_ref[pl.ds(r0, c), cs].astype(BF16)
            vv = v_ref[pl.ds(r0, c), cs].astype(BF16)
            ej = e_ref[j, :, cs]
            st = st_ref[h] if seq else s0_ref[j, h].T
            o_int = lax.dot_general(qd, st.astype(BF16), NT_DIMS, preferred_element_type=F32)
            kv_t = lax.dot_general(vv, kd, TN_DIMS, preferred_element_type=F32)
            st_new = st * ej + kv_t
            oi_ref[pl.ds(r0, c), cs] = oi_ref[pl.ds(r0, c), cs] + o_int
            if seq:
                st_ref[h] = st_new
            else:
                so_ref[j, h] = st_new.T
        return carry

    lax.fori_loop(0, nsub, body, 0)

    if seq:
        @pl.when(ti == pl.num_programs(1) - 1)
        def _():
            for h in range(HG_HEADS):
                so_ref[h] = st_ref[h].T

    o = oi_ref[...]
    gn = gn_ref[...]
    outs = []
    for h in range(HG_HEADS):
        oh = o[:, h * hd:(h + 1) * hd]
        outs.append(oh * lax.rsqrt(jnp.mean(oh * oh, axis=-1, keepdims=True) + RMS_EPS))
    y = jnp.concatenate(outs, axis=1) * gn * (g * _sigmoid(g))
    mix = _dot(y.astype(BF16), wout_ref[...])
    o_ref[...] = _layer_norm(ALPHA * x + mix, lnw_ref[...], lnb_ref[...])


def _hgrn_scratch(rows, c):
    d = D_MODEL
    return [pltpu.VMEM((rows, d), F32), pltpu.VMEM((rows, d), F32), pltpu.VMEM((rows, d), F32),
            pltpu.VMEM((rows // c, 1, d), F32), pltpu.VMEM((rows, d), F32)]


def _hgrn_prompt_call(x, win, lb, gn, wout, lnw, lnb, *, tc):
    bsz, t_len, d = x.shape
    hd = HG_HEAD_DIM
    kern = functools.partial(_hgrn_kernel, rows=tc, c=HG_CHUNK, seq=True)
    xspec = pl.BlockSpec((None, tc, d), lambda b, t: (b, t, 0))
    return pl.pallas_call(
        kern,
        grid=(bsz, t_len // tc),
        in_specs=[xspec, _resident(win.shape), _resident(lb.shape), _resident(gn.shape),
                  _resident(wout.shape), _resident(lnw.shape), _resident(lnb.shape)],
        out_specs=[xspec, pl.BlockSpec((None, HG_HEADS, hd, hd), lambda b, t: (b, 0, 0, 0))],
        out_shape=[jax.ShapeDtypeStruct(x.shape, F32),
                   jax.ShapeDtypeStruct((bsz, HG_HEADS, hd, hd), F32)],
        scratch_shapes=[pltpu.VMEM((HG_HEADS, hd, hd), F32)] + _hgrn_scratch(tc, HG_CHUNK),
        compiler_params=pltpu.CompilerParams(dimension_semantics=("parallel", "arbitrary"),
                                             vmem_limit_bytes=VMEM_LIMIT_BYTES),
        name="hgrn_mixer_prompt",
    )(x, win, lb, gn, wout, lnw, lnb)


def _hgrn_sample_call(x, s0, win, lb, gn, wout, lnw, lnb, *, bc):
    bsz, t_len, d = x.shape
    hd = HG_HEAD_DIM
    rows = bc * t_len
    kern = functools.partial(_hgrn_kernel, rows=rows, c=t_len, seq=False)
    xspec = pl.BlockSpec((rows, d), lambda i: (i, 0))
    sspec = pl.BlockSpec((bc, HG_HEADS, hd, hd), lambda i: (i, 0, 0, 0))
    xo, so = pl.pallas_call(
        kern,
        grid=(bsz // bc,),
        in_specs=[xspec, sspec, _resident(win.shape), _resident(lb.shape), _resident(gn.shape),
                  _resident(wout.shape), _resident(lnw.shape), _resident(lnb.shape)],
        out_specs=[xspec, sspec],
        out_shape=[jax.ShapeDtypeStruct((bsz * t_len, d), F32), jax.ShapeDtypeStruct(s0.shape, F32)],
        scratch_shapes=_hgrn_scratch(rows, t_len),
        compiler_params=pltpu.CompilerParams(dimension_semantics=("parallel",),
                                             vmem_limit_bytes=VMEM_LIMIT_BYTES),
        name="hgrn_mixer_sample",
    )(x.reshape(bsz * t_len, d), s0, win, lb, gn, wout, lnw, lnb)
    return xo.reshape(bsz, t_len, d), so


def _tile(n, target):
    t = min(n, target)
    while n % t:
        t //= 2
    return t


def kernel(x_prompt, x_sample, state_hgrn, state_s5_re, state_s5_im, p_prompt, p_sample, hg_w_in, hg_lower_bounds, hg_gnorm_w, hg_w_out, s5_a_re, s5_a_im, s5_b_re, s5_b_im, s5_c_re, s5_c_im, s5_d, s5_log_step, s5_w_glu, ln_mix_w, ln_mix_b, ffn_w_gate_up, ffn_w_down, ln_ffn_w, ln_ffn_b, ple_w_proj, ple_w_gate, ple_norm_w):
    bp, tp, d = x_prompt.shape
    bs, ts, _ = x_sample.shape
    depth = ln_mix_w.shape[0]
    ple = p_prompt.shape[-1]

    lb_soft = jax.nn.softmax(hg_lower_bounds.astype(F32), axis=0)
    lb_all = jnp.cumsum(lb_soft, axis=0) - lb_soft[0]

    xp, xs = x_prompt, x_sample
    hg_p, re_p, im_p, hg_s, re_s, im_s = [], [], [], [], [], []
    row2 = lambda a: a.reshape(1, -1)
    for i in range(depth):
        j = i // 2
        lnw, lnb = row2(ln_mix_w[i]), row2(ln_mix_b[i])
        if i % 2 == 0:
            win = hg_w_in[j].astype(BF16)
            wout = hg_w_out[j].astype(BF16)
            lb = row2(lb_all[j])
            gn = row2(jnp.tile(hg_gnorm_w[j], HG_HEADS))
            xp, sp = _hgrn_prompt_call(xp, win, lb, gn, wout, lnw, lnb, tc=_tile(tp, 256))
            xs, ss = _hgrn_sample_call(xs, state_hgrn[j], win, lb, gn, wout, lnw, lnb, bc=_tile(bs, 16))
            hg_p.append(sp)
            hg_s.append(ss)
        else:
            lamr, lami, bdr, bdi, cd = _s5_weights(s5_a_re[j], s5_a_im[j], s5_b_re[j], s5_b_im[j],
                                                   s5_c_re[j], s5_c_im[j], s5_log_step[j])
            dsk = row2(s5_d[j])
            wglu = s5_w_glu[j].astype(BF16)
            xo, hr, hi = _s5_call(xp.reshape(bp, tp * d), None, None, lamr, lami, bdr, bdi, cd, dsk, wglu,
                                  lnw, lnb, bt=bp, tc=_tile(tp, 256 // bp))
            xp = xo.reshape(bp, tp, d)
            re_p.append(hr.reshape(bp, S5_GROUPS, S5_STATE))
            im_p.append(hi.reshape(bp, S5_GROUPS, S5_STATE))
            xo, hr, hi = _s5_call(xs.reshape(bs, ts * d), state_s5_re[j].reshape(bs, S5_COLS),
                                  state_s5_im[j].reshape(bs, S5_COLS), lamr, lami, bdr, bdi, cd, dsk, wglu,
                                  lnw, lnb, bt=_tile(bs, 256 // ts), tc=ts)
            xs = xo.reshape(bs, ts, d)
            re_s.append(hr.reshape(bs, S5_GROUPS, S5_STATE))
            im_s.append(hi.reshape(bs, S5_GROUPS, S5_STATE))
        dense_w = (ffn_w_gate_up[i].astype(BF16), ffn_w_down[i].astype(BF16), row2(ln_ffn_w[i]),
                   row2(ln_ffn_b[i]), ple_w_proj[i].astype(BF16), ple_w_gate[i].astype(BF16),
                   row2(ple_norm_w[i]))
        xp = _dense_call(xp.reshape(bp * tp, d), p_prompt[i].reshape(bp * tp, ple), *dense_w,
                         tm=_tile(bp * tp, 512)).reshape(bp, tp, d)
        xs = _dense_call(xs.reshape(bs * ts, d), p_sample[i].reshape(bs * ts, ple), *dense_w,
                         tm=_tile(bs * ts, 512)).reshape(bs, ts, d)
    return (xp, xs, jnp.stack(hg_p), jnp.stack(re_p), jnp.stack(im_p),
            jnp.stack(hg_s), jnp.stack(re_s), jnp.stack(im_s))
```

```python
import functools
import math

import jax
import jax.numpy as jnp
from jax import lax
from jax.experimental import pallas as pl
from jax.experimental.pallas import tpu as pltpu

F32 = jnp.float32
BF16 = jnp.bfloat16

D_MODEL = 1024
HG_HEAD_DIM = 128
HG_HEADS = D_MODEL // HG_HEAD_DIM
HG_QK_SCALE = HG_HEAD_DIM ** -0.5
HG_CHUNK = 16
S5_GROUP = 16
S5_GROUPS = D_MODEL // S5_GROUP
S5_STATE = 64
S5_COLS = S5_GROUPS * S5_STATE
DEPTH = 4
ALPHA = (2 * DEPTH) ** 0.25
LN_EPS = 1e-5
RMS_EPS = 1e-6

V7X_VMEM_BYTES = 64 * 1024 * 1024
VMEM_LIMIT_BYTES = V7X_VMEM_BYTES - 8 * 1024 * 1024
MXU_TILE = 256
LANES = 128
VREG_ELEMS = 8 * LANES

S5_GROUPS_PER_TILE = MXU_TILE // S5_GROUP
S5_KT = S5_GROUPS // S5_GROUPS_PER_TILE
S5_KT_COLS = S5_GROUPS_PER_TILE * S5_STATE

NT_DIMS = (((1,), (1,)), ((), ()))
TN_DIMS = (((0,), (0,)), ((), ()))


def _dot(a, b):
    return jnp.dot(a, b, preferred_element_type=F32)


def _sigmoid(x):
    return 1.0 / (1.0 + jnp.exp(-x))


def _layer_norm(x, w, b):
    mu = jnp.mean(x, axis=-1, keepdims=True)
    xc = x - mu
    var = jnp.mean(xc * xc, axis=-1, keepdims=True)
    return xc * lax.rsqrt(var + LN_EPS) * w + b


def _rms_norm(x, w):
    return x * lax.rsqrt(jnp.mean(x * x, axis=-1, keepdims=True) + RMS_EPS) * w


def _resident(shape):
    nd = len(shape)
    return pl.BlockSpec(shape, lambda *_: (0,) * nd, pipeline_mode=pl.Buffered(1))


def _layer_block(shape, layer):
    nd = len(shape)
    return pl.BlockSpec((None,) + tuple(shape[1:]), lambda *_: (layer,) + (0,) * (nd - 1),
                        pipeline_mode=pl.Buffered(1))


def _dense_kernel(x_ref, p_ref, wgu_ref, wd_ref, lnw_ref, lnb_ref, wproj_ref, wgate_ref, nw_ref, o_ref,
                  *, d_ff, ff_chunk):
    x = x_ref[...]
    xb = x.astype(BF16)
    acc = jnp.zeros(x.shape, F32)
    for j in range(d_ff // ff_chunk):
        lo = j * ff_chunk
        g = _dot(xb, wgu_ref[:, lo:lo + ff_chunk])
        u = _dot(xb, wgu_ref[:, d_ff + lo:d_ff + lo + ff_chunk])
        h = (g * _sigmoid(g)) * u
        acc = acc + _dot(h.astype(BF16), wd_ref[lo:lo + ff_chunk, :])
    x2 = _layer_norm(ALPHA * x + acc, lnw_ref[...], lnb_ref[...])
    e = _dot(p_ref[...].astype(BF16), wproj_ref[...]) * _sigmoid(_dot(x2.astype(BF16), wgate_ref[...]))
    o_ref[...] = x2 + _rms_norm(e, nw_ref[...])


def _dense_call(x, p_all, layer, wgu, wd, lnw, lnb, wproj, wgate, nw, *, tm):
    n, d = x.shape
    d_ff = wd.shape[1]
    ple = p_all.shape[-1]
    row = lambda i: (i, 0)
    kern = functools.partial(_dense_kernel, d_ff=d_ff, ff_chunk=MXU_TILE)
    return pl.pallas_call(
        kern,
        grid=(n // tm,),
        in_specs=[pl.BlockSpec((tm, d), row), pl.BlockSpec((None, tm, ple), lambda i: (layer, i, 0)),
                  _layer_block(wgu.shape, layer), _layer_block(wd.shape, layer),
                  _layer_block(lnw.shape, layer), _layer_block(lnb.shape, layer),
                  _layer_block(wproj.shape, layer), _layer_block(wgate.shape, layer),
                  _layer_block(nw.shape, layer)],
        out_specs=pl.BlockSpec((tm, d), row),
        out_shape=jax.ShapeDtypeStruct((n, d), F32),
        compiler_params=pltpu.CompilerParams(dimension_semantics=("parallel",),
                                             vmem_limit_bytes=VMEM_LIMIT_BYTES),
        name="dense_ffn_ple",
    )(x, p_all, wgu, wd, lnw, lnb, wproj, wgate, nw)


def _s5_kernel(*refs, bt, tc, has_state):
    if has_state:
        x_ref, h0r_ref, h0i_ref = refs[:3]
        refs = refs[3:]
    else:
        x_ref = refs[0]
        refs = refs[1:]
    (lamr_ref, lami_ref, bbr_ref, bbi_ref, cc_ref, dsk_ref, wglu_ref, lnw_ref, lnb_ref,
     o_ref, hro_ref, hio_ref,
     bur_ref, bui_ref, hsr_ref, hsi_ref, bdr_ref, bdi_ref, cdt_ref, slab_ref) = refs
    d = D_MODEL
    rows = bt * tc
    n_slab = d // LANES
    ti = pl.program_id(1)

    @pl.when(ti == 0)
    def _():
        if has_state:
            hsr_ref[...] = h0r_ref[...]
            hsi_ref[...] = h0i_ref[...]
        else:
            hsr_ref[...] = jnp.zeros(hsr_ref.shape, F32)
            hsi_ref[...] = jnp.zeros(hsi_ref.shape, F32)
        rep = MXU_TILE // S5_GROUP
        gi = lax.broadcasted_iota(jnp.int32, (MXU_TILE, S5_KT_COLS), 0) // S5_GROUP
        ci = lax.broadcasted_iota(jnp.int32, (MXU_TILE, S5_KT_COLS), 1) // S5_STATE
        in_mask = gi == ci
        out_mask = jnp.concatenate([in_mask, in_mask], axis=1)
        for kt in range(S5_KT):
            bdr_ref[kt] = jnp.where(in_mask, jnp.concatenate([bbr_ref[kt]] * rep, axis=0), 0.0).astype(BF16)
            bdi_ref[kt] = jnp.where(in_mask, jnp.concatenate([bbi_ref[kt]] * rep, axis=0), 0.0).astype(BF16)
            cdt_ref[kt] = jnp.where(out_mask, jnp.concatenate([cc_ref[kt]] * rep, axis=0), 0.0).astype(BF16)

    for s in range(n_slab):
        for b in range(bt):
            slab_ref[s, pl.ds(b, tc, stride=bt), :] = x_ref[b, :, s * LANES:(s + 1) * LANES]
    u = jnp.concatenate([slab_ref[s] for s in range(n_slab)], axis=1)
    ub = u.astype(BF16)
    for kt in range(S5_KT):
        uk = ub[:, kt * MXU_TILE:(kt + 1) * MXU_TILE]
        bur_ref[:, kt * S5_KT_COLS:(kt + 1) * S5_KT_COLS] = _dot(uk, bdr_ref[kt])
        bui_ref[:, kt * S5_KT_COLS:(kt + 1) * S5_KT_COLS] = _dot(uk, bdi_ref[kt])

    cols = 4 * VREG_ELEMS // bt
    for s in range(S5_COLS // cols):
        cs = slice(s * cols, (s + 1) * cols)
        lr = jnp.broadcast_to(lamr_ref[:, cs], (bt, cols))
        li = jnp.broadcast_to(lami_ref[:, cs], (bt, cols))
        hr = hsr_ref[:, cs]
        hi = hsi_ref[:, cs]
        for t in range(tc):
            rs = slice(t * bt, (t + 1) * bt)
            nr = lr * hr - li * hi + bur_ref[rs, cs]
            ni = lr * hi + li * hr + bui_ref[rs, cs]
            bur_ref[rs, cs] = nr
            bui_ref[rs, cs] = ni
            hr, hi = nr, ni
        hsr_ref[:, cs] = hr
        hsi_ref[:, cs] = hi
    hro_ref[...] = hsr_ref[...]
    hio_ref[...] = hsi_ref[...]

    ys = []
    for nt in range(S5_KT):
        hcat = jnp.concatenate([bur_ref[:, nt * S5_KT_COLS:(nt + 1) * S5_KT_COLS],
                                bui_ref[:, nt * S5_KT_COLS:(nt + 1) * S5_KT_COLS]], axis=1).astype(BF16)
        ys.append(lax.dot_general(hcat, cdt_ref[nt], NT_DIMS, preferred_element_type=F32))
    y = jnp.concatenate(ys, axis=1) + dsk_ref[...] * u
    y = y * (0.5 * (1.0 + jnp.tanh(math.sqrt(2.0 / math.pi) * (y + 0.044715 * (y * y * y)))))
    z = _dot(y.astype(BF16), wglu_ref[...])
    mix = z[:, :d] * _sigmoid(z[:, d:])
    xn = _layer_norm(ALPHA * u + mix, lnw_ref[...], lnb_ref[...])
    for s in range(n_slab):
        slab_ref[s] = xn[:, s * LANES:(s + 1) * LANES]
    for b in range(bt):
        for s in range(n_slab):
            o_ref[b, :, s * LANES:(s + 1) * LANES] = slab_ref[s, pl.ds(b, tc, stride=bt), :]


def _s5_call(x, h0, layer, hprev, lamr, lami, bbr, bbi, cc, dsk, wglu, lnw, lnb, mix_layer, *, bt, tc):
    bsz, t_len, d = x.shape
    n_layers = lamr.shape[0]
    has_state = h0 is not None
    rows = bt * tc
    xspec = pl.BlockSpec((bt, tc, d), lambda b, t: (b, t, 0))
    sspec = pl.BlockSpec((None, bt, S5_COLS), lambda b, t: (layer, b, 0))
    any_spec = pl.BlockSpec(memory_space=pl.ANY)
    in_specs = [xspec] + ([sspec, sspec] if has_state else []) + [
        _layer_block(lamr.shape, layer), _layer_block(lami.shape, layer), _layer_block(bbr.shape, layer),
        _layer_block(bbi.shape, layer), _layer_block(cc.shape, layer), _layer_block(dsk.shape, layer),
        _layer_block(wglu.shape, layer), _layer_block(lnw.shape, mix_layer), _layer_block(lnb.shape, mix_layer)]
    args = [x] + (list(h0) if has_state else []) + [lamr, lami, bbr, bbi, cc, dsk, wglu, lnw, lnb]
    aliases = {}
    if hprev is not None:
        aliases = {len(args): 1, len(args) + 1: 2}
        in_specs += [any_spec, any_spec]
        args += list(hprev)
    kern = functools.partial(_s5_kernel_entry, bt=bt, tc=tc, has_state=has_state, n_extra=len(aliases))
    st_shape = jax.ShapeDtypeStruct((n_layers, bsz, S5_COLS), F32)
    return pl.pallas_call(
        kern,
        grid=(bsz // bt, t_len // tc),
        in_specs=in_specs,
        out_specs=[xspec, sspec, sspec],
        out_shape=[jax.ShapeDtypeStruct(x.shape, F32), st_shape, st_shape],
        scratch_shapes=[pltpu.VMEM((rows, S5_COLS), F32), pltpu.VMEM((rows, S5_COLS), F32),
                        pltpu.VMEM((bt, S5_COLS), F32), pltpu.VMEM((bt, S5_COLS), F32),
                        pltpu.VMEM((S5_KT, MXU_TILE, S5_KT_COLS), BF16),
                        pltpu.VMEM((S5_KT, MXU_TILE, S5_KT_COLS), BF16),
                        pltpu.VMEM((S5_KT, MXU_TILE, 2 * S5_KT_COLS), BF16),
                        pltpu.VMEM((d // LANES, rows, LANES), F32)],
        input_output_aliases=aliases,
        compiler_params=pltpu.CompilerParams(dimension_semantics=("parallel", "arbitrary"),
                                             vmem_limit_bytes=VMEM_LIMIT_BYTES),
        name="s5_mixer",
    )(*args)


def _s5_kernel_entry(*refs, bt, tc, has_state, n_extra):
    n_in = 1 + (2 if has_state else 0) + 9
    _s5_kernel(*(refs[:n_in] + refs[n_in + n_extra:]), bt=bt, tc=tc, has_state=has_state)


def _s5_params(a_re, a_im, b_re, b_im, c_re, c_im, log_step):
    n_l = a_re.shape[0]
    delta = jnp.exp(log_step)[..., None]
    er = jnp.exp(a_re * delta)
    lr = er * jnp.cos(a_im * delta)
    li = er * jnp.sin(a_im * delta)
    den = a_re * a_re + a_im * a_im
    qr = ((lr - 1.0) * a_re + li * a_im) / den
    qi = (li * a_re - (lr - 1.0) * a_im) / den
    bbr = qr[..., None] * b_re - qi[..., None] * b_im
    bbi = qr[..., None] * b_im + qi[..., None] * b_re

    def compact_in(bb):
        w = bb.reshape(n_l, S5_KT, S5_GROUPS_PER_TILE, S5_STATE, S5_GROUP).transpose(0, 1, 4, 2, 3)
        return w.reshape(n_l, S5_KT, S5_GROUP, S5_KT_COLS)

    def compact_out(cm):
        w = cm.reshape(n_l, S5_KT, S5_GROUPS_PER_TILE, S5_GROUP, S5_STATE).transpose(0, 1, 3, 2, 4)
        return w.reshape(n_l, S5_KT, S5_GROUP, S5_KT_COLS)

    cc = jnp.concatenate([compact_out(c_re), -compact_out(c_im)], axis=-1)
    return (lr.reshape(n_l, 1, S5_COLS), li.reshape(n_l, 1, S5_COLS), compact_in(bbr), compact_in(bbi), cc)


def _hgrn_kernel(*refs, rows, c, seq):
    if seq:
        (x_ref, win_ref, lb_ref, gn_ref, wout_ref, lnw_ref, lnb_ref,
         o_ref, so_ref, st_ref, qd_ref, kd_ref, v_ref, e_ref, oi_ref) = refs
        s0_ref = None
    else:
        (x_ref, s0_ref, win_ref, lb_ref, gn_ref, wout_ref, lnw_ref, lnb_ref,
         o_ref, so_ref, qd_ref, kd_ref, v_ref, e_ref, oi_ref) = refs
        st_ref = None
    d = D_MODEL
    hd = HG_HEAD_DIM
    nsub = rows // c
    lg = c.bit_length() - 1

    if seq:
        ti = pl.program_id(1)

        @pl.when(ti == 0)
        def _():
            st_ref[...] = jnp.zeros(st_ref.shape, F32)

    x = x_ref[...]
    proj = _dot(x.astype(BF16), win_ref[...])
    q = proj[:, :d]
    fz = proj[:, d:2 * d]
    v = proj[:, 2 * d:3 * d]
    g = proj[:, 3 * d:]
    lb = lb_ref[...]
    f = lb + (1.0 - lb) * _sigmoid(fz)
    logf = jnp.log(f)
    kk = 1.0 - f
    qs = q * _sigmoid(q) * HG_QK_SCALE

    ri = lax.broadcasted_iota(jnp.int32, (rows, rows), 0)
    ci = lax.broadcasted_iota(jnp.int32, (rows, rows), 1)
    causal = jnp.logical_and((ri >> lg) == (ci >> lg), ci <= ri)
    tri = jnp.where(causal, 1.0, 0.0).astype(BF16)
    l_hi = logf.astype(BF16)
    l_lo = (logf - l_hi.astype(F32)).astype(BF16)
    b = _dot(tri, l_hi) + _dot(tri, l_lo)
    b3 = b.reshape(nsub, c, d)
    bm = b3[:, c // 2 - 1:c // 2, :]
    bl = b3[:, c - 1:c, :]
    d1 = (b3 - bm).reshape(rows, d)
    qh = (qs * jnp.exp(d1)).astype(BF16)
    kh = (kk * jnp.exp(-d1)).astype(BF16)
    qd_ref[...] = qs * jnp.exp(b)
    kd_ref[...] = kk * jnp.exp((bl - b3).reshape(rows, d))
    v_ref[...] = v
    e_ref[...] = jnp.exp(bl)
    vb = v.astype(BF16)

    for h in range(HG_HEADS):
        cs = slice(h * hd, (h + 1) * hd)
        att = lax.dot_general(qh[:, cs], kh[:, cs], NT_DIMS, preferred_element_type=F32)
        att = jnp.where(causal, att, 0.0)
        oi_ref[:, cs] = _dot(att.astype(BF16), vb[:, cs])

    def body(j, carry):
        r0 = j * c if isinstance(j, int) else pl.multiple_of(j * c, c)
        for h in range(HG_HEADS):
            cs = slice(h * hd, (h + 1) * hd)
            qd = qd_ref[pl.ds(r0, c), cs].astype(BF16)
            kd = kd_ref[pl.ds(r0, c), cs].astype(BF16)
            vv = v_ref[pl.ds(r0, c), cs].astype(BF16)
            ej = e_ref[j, :, cs]
            st = st_ref[h] if seq else s0_ref[j, h].T
            o_int = lax.dot_general(qd, st.astype(BF16), NT_DIMS, preferred_element_type=F32)
            kv_t = lax.dot_general(vv, kd, TN_DIMS, preferred_element_type=F32)
            st_new = st * ej + kv_t
            oi_ref[pl.ds(r0, c), cs] = oi_ref[pl.ds(r0, c), cs] + o_int
            if seq:
                st_ref[h] = st_new
            else:
                so_ref[j, h] = st_new.T
        return carry

    if seq:
        for j in range(nsub):
            body(j, 0)
    else:
        lax.fori_loop(0, nsub, body, 0)

    if seq:
        @pl.when(ti == pl.num_programs(1) - 1)
        def _():
            for h in range(HG_HEADS):
                so_ref[h] = st_ref[h].T

    o = oi_ref[...]
    outs = []
    for h in range(HG_HEADS):
        oh = o[:, h * hd:(h + 1) * hd]
        outs.append(oh * lax.rsqrt(jnp.mean(oh * oh, axis=-1, keepdims=True) + RMS_EPS))
    y = jnp.concatenate(outs, axis=1) * gn_ref[...] * (g * _sigmoid(g))
    mix = _dot(y.astype(BF16), wout_ref[...])
    o_ref[...] = _layer_norm(ALPHA * x + mix, lnw_ref[...], lnb_ref[...])


def _hgrn_kernel_entry(*refs, n_in, n_extra, **kw):
    _hgrn_kernel(*(refs[:n_in] + refs[n_in + n_extra:]), **kw)


def _hgrn_scratch(rows, c):
    d = D_MODEL
    return [pltpu.VMEM((rows, d), F32), pltpu.VMEM((rows, d), F32), pltpu.VMEM((rows, d), F32),
            pltpu.VMEM((rows // c, 1, d), F32), pltpu.VMEM((rows, d), F32)]


def _hgrn_call(x, s0, layer, sprev, win, lb, gn, wout, lnw, lnb, mix_layer, *, tile):
    bsz, t_len, d = x.shape
    hd = HG_HEAD_DIM
    n_layers = win.shape[0]
    seq = s0 is None
    weights = [win, lb, gn, wout]
    w_specs = [_layer_block(w.shape, layer) for w in weights] + [
        _layer_block(lnw.shape, mix_layer), _layer_block(lnb.shape, mix_layer)]
    any_spec = pl.BlockSpec(memory_space=pl.ANY)
    st_shape = jax.ShapeDtypeStruct((n_layers, bsz, HG_HEADS, hd, hd), F32)
    if seq:
        rows, c = tile, HG_CHUNK
        grid = (bsz, t_len // tile)
        xspec = pl.BlockSpec((None, tile, d), lambda b, t: (b, t, 0))
        sspec = pl.BlockSpec((None, None, HG_HEADS, hd, hd), lambda b, t: (layer, b, 0, 0, 0))
        in_specs, args = [xspec] + w_specs, [x] + weights + [lnw, lnb]
        scratch = [pltpu.VMEM((HG_HEADS, hd, hd), F32)] + _hgrn_scratch(rows, c)
        sem = ("parallel", "arbitrary")
        x_in, x_shape = x, x.shape
    else:
        rows, c = tile * t_len, t_len
        grid = (bsz // tile,)
        xspec = pl.BlockSpec((rows, d), lambda i: (i, 0))
        sspec = pl.BlockSpec((None, tile, HG_HEADS, hd, hd), lambda i: (layer, i, 0, 0, 0))
        x_in, x_shape = x.reshape(bsz * t_len, d), (bsz * t_len, d)
        in_specs, args = [xspec, sspec] + w_specs, [x_in, s0] + weights + [lnw, lnb]
        scratch = _hgrn_scratch(rows, c)
        sem = ("parallel",)
    n_in = len(args)
    aliases = {}
    if sprev is not None:
        aliases = {n_in: 1}
        in_specs, args = in_specs + [any_spec], args + [sprev]
    if seq:
        args[0] = x_in
    kern = functools.partial(_hgrn_kernel_entry, n_in=n_in, n_extra=len(aliases), rows=rows, c=c, seq=seq)
    xo, so = pl.pallas_call(
        kern,
        grid=grid,
        in_specs=in_specs,
        out_specs=[xspec, sspec],
        out_shape=[jax.ShapeDtypeStruct(x_shape, F32), st_shape],
        scratch_shapes=scratch,
        input_output_aliases=aliases,
        compiler_params=pltpu.CompilerParams(dimension_semantics=sem, vmem_limit_bytes=VMEM_LIMIT_BYTES),
        name="hgrn_mixer_prompt" if seq else "hgrn_mixer_sample",
    )(*args)
    return xo.reshape(bsz, t_len, d), so


def _tile(n, target):
    t = min(n, target)
    while n % t:
        t //= 2
    return t


def kernel(x_prompt, x_sample, state_hgrn, state_s5_re, state_s5_im, p_prompt, p_sample, hg_w_in, hg_lower_bounds, hg_gnorm_w, hg_w_out, s5_a_re, s5_a_im, s5_b_re, s5_b_im, s5_c_re, s5_c_im, s5_d, s5_log_step, s5_w_glu, ln_mix_w, ln_mix_b, ffn_w_gate_up, ffn_w_down, ln_ffn_w, ln_ffn_b, ple_w_proj, ple_w_gate, ple_norm_w):
    bp, tp, d = x_prompt.shape
    bs, ts, _ = x_sample.shape
    depth = ln_mix_w.shape[0]
    ple = p_prompt.shape[-1]
    n_s5 = s5_a_re.shape[0]
    row3 = lambda a: a.reshape(a.shape[0], 1, -1)

    lb_soft = jax.nn.softmax(hg_lower_bounds.astype(F32), axis=0)
    lb_all = row3(jnp.cumsum(lb_soft, axis=0) - lb_soft[0])
    gn_all = row3(jnp.tile(hg_gnorm_w, (1, HG_HEADS)))
    hg_win, hg_wout = hg_w_in.astype(BF16), hg_w_out.astype(BF16)
    lamr, lami, bbr, bbi, cc = _s5_params(s5_a_re, s5_a_im, s5_b_re, s5_b_im, s5_c_re, s5_c_im, s5_log_step)
    dsk_all, wglu_all = row3(s5_d), s5_w_glu.astype(BF16)
    lnm_w, lnm_b = row3(ln_mix_w), row3(ln_mix_b)
    dense_w = (ffn_w_gate_up.astype(BF16), ffn_w_down.astype(BF16), row3(ln_ffn_w), row3(ln_ffn_b),
               ple_w_proj.astype(BF16), ple_w_gate.astype(BF16), row3(ple_norm_w))
    pp = p_prompt.reshape(depth, bp * tp, ple)
    ps = p_sample.reshape(depth, bs * ts, ple)
    h0_s = (state_s5_re.reshape(n_s5, bs, S5_COLS), state_s5_im.reshape(n_s5, bs, S5_COLS))

    xp, xs = x_prompt, x_sample
    hg_p = hg_s = s5_p = s5_s = None
    for i in range(depth):
        j = i // 2
        if i % 2 == 0:
            hg_args = (hg_win, lb_all, gn_all, hg_wout, lnm_w, lnm_b, i)
            xp, hg_p = _hgrn_call(xp, None, j, hg_p, *hg_args, tile=_tile(tp, 256))
            xs, hg_s = _hgrn_call(xs, state_hgrn, j, hg_s, *hg_args, tile=_tile(bs, 16))
        else:
            s5_args = (lamr, lami, bbr, bbi, cc, dsk_all, wglu_all, lnm_w, lnm_b, i)
            xp, *s5_p = _s5_call(xp, None, j, s5_p, *s5_args, bt=bp, tc=_tile(tp, 256 // bp))
            xs, *s5_s = _s5_call(xs, h0_s, j, s5_s, *s5_args, bt=_tile(bs, 256 // ts), tc=ts)
        xp = _dense_call(xp.reshape(bp * tp, d), pp, i, *dense_w, tm=_tile(bp * tp, 512)).reshape(bp, tp, d)
        xs = _dense_call(xs.reshape(bs * ts, d), ps, i, *dense_w, tm=_tile(bs * ts, 512)).reshape(bs, ts, d)
    s5_shape = lambda a, n: a.reshape(n_s5, n, S5_GROUPS, S5_STATE)
    return (xp, xs, hg_p, s5_shape(s5_p[0], bp), s5_shape(s5_p[1], bp),
            hg_s, s5_shape(s5_s[0], bs), s5_shape(s5_s[1], bs))
```

```python
import functools
import math

import jax
import jax.numpy as jnp
from jax import lax
from jax.experimental import pallas as pl
from jax.experimental.pallas import tpu as pltpu

F32 = jnp.float32
BF16 = jnp.bfloat16

D_MODEL = 1024
HG_HEAD_DIM = 128
HG_HEADS = D_MODEL // HG_HEAD_DIM
HG_QK_SCALE = HG_HEAD_DIM ** -0.5
HG_CHUNK = 16
S5_GROUP = 16
S5_GROUPS = D_MODEL // S5_GROUP
S5_STATE = 64
S5_COLS = S5_GROUPS * S5_STATE
DEPTH = 4
ALPHA = (2 * DEPTH) ** 0.25
LN_EPS = 1e-5
RMS_EPS = 1e-6

V7X_VMEM_BYTES = 64 * 1024 * 1024
VMEM_LIMIT_BYTES = V7X_VMEM_BYTES - 8 * 1024 * 1024
MXU_TILE = 256
LANES = 128
VREG_ELEMS = 8 * LANES

S5_GROUPS_PER_TILE = MXU_TILE // S5_GROUP
S5_KT = S5_GROUPS // S5_GROUPS_PER_TILE
S5_KT_COLS = S5_GROUPS_PER_TILE * S5_STATE

NT_DIMS = (((1,), (1,)), ((), ()))
TN_DIMS = (((0,), (0,)), ((), ()))


def _dot(a, b):
    return jnp.dot(a, b, preferred_element_type=F32)


def _sigmoid(x):
    return 0.5 * jnp.tanh(0.5 * x) + 0.5


def _layer_norm(x, w, b):
    mu = jnp.mean(x, axis=-1, keepdims=True)
    xc = x - mu
    var = jnp.mean(xc * xc, axis=-1, keepdims=True)
    return xc * lax.rsqrt(var + LN_EPS) * w + b


def _rms_norm(x, w):
    return x * lax.rsqrt(jnp.mean(x * x, axis=-1, keepdims=True) + RMS_EPS) * w


def _bf16_split_rows(e):
    hi = e.astype(BF16).astype(F32)
    r1 = e - hi
    mid = r1.astype(BF16).astype(F32)
    lo = (r1 - mid).astype(BF16).astype(F32)
    row = lax.broadcasted_iota(jnp.int32, (8, e.shape[1]), 0)
    return jnp.where(row == 0, hi, jnp.where(row == 1, mid, jnp.where(row == 2, lo, 0.0))).astype(BF16)


def _resident(shape):
    nd = len(shape)
    return pl.BlockSpec(shape, lambda *_: (0,) * nd, pipeline_mode=pl.Buffered(1))


def _layer_block(shape, layer):
    nd = len(shape)
    return pl.BlockSpec((None,) + tuple(shape[1:]), lambda *_: (layer,) + (0,) * (nd - 1),
                        pipeline_mode=pl.Buffered(1))


def _dense_kernel(x_ref, p_ref, wgu_ref, wd_ref, lnw_ref, lnb_ref, wproj_ref, wgate_ref, nw_ref, o_ref,
                  *, d_ff, ff_chunk):
    x = x_ref[...]
    xb = x.astype(BF16)
    acc = jnp.zeros(x.shape, F32)
    for j in range(d_ff // ff_chunk):
        lo = j * ff_chunk
        g = _dot(xb, wgu_ref[:, lo:lo + ff_chunk])
        u = _dot(xb, wgu_ref[:, d_ff + lo:d_ff + lo + ff_chunk])
        h = (g * _sigmoid(g)) * u
        acc = acc + _dot(h.astype(BF16), wd_ref[lo:lo + ff_chunk, :])
    x2 = _layer_norm(ALPHA * x + acc, lnw_ref[...], lnb_ref[...])
    e = _dot(p_ref[...].astype(BF16), wproj_ref[...]) * _sigmoid(_dot(x2.astype(BF16), wgate_ref[...]))
    o_ref[...] = x2 + _rms_norm(e, nw_ref[...])


def _dense_call(x, p_all, layer, wgu, wd, lnw, lnb, wproj, wgate, nw, *, tm):
    n, d = x.shape
    d_ff = wd.shape[1]
    ple = p_all.shape[-1]
    row = lambda i: (i, 0)
    kern = functools.partial(_dense_kernel, d_ff=d_ff, ff_chunk=MXU_TILE)
    return pl.pallas_call(
        kern,
        grid=(n // tm,),
        in_specs=[pl.BlockSpec((tm, d), row), pl.BlockSpec((None, tm, ple), lambda i: (layer, i, 0)),
                  _layer_block(wgu.shape, layer), _layer_block(wd.shape, layer),
                  _layer_block(lnw.shape, layer), _layer_block(lnb.shape, layer),
                  _layer_block(wproj.shape, layer), _layer_block(wgate.shape, layer),
                  _layer_block(nw.shape, layer)],
        out_specs=pl.BlockSpec((tm, d), row),
        out_shape=jax.ShapeDtypeStruct((n, d), F32),
        compiler_params=pltpu.CompilerParams(dimension_semantics=("parallel",),
                                             vmem_limit_bytes=VMEM_LIMIT_BYTES),
        name="dense_ffn_ple",
    )(x, p_all, wgu, wd, lnw, lnb, wproj, wgate, nw)


def _s5_kernel(*refs, bt, tc, has_state):
    if has_state:
        x_ref, h0r_ref, h0i_ref = refs[:3]
        refs = refs[3:]
    else:
        x_ref = refs[0]
        refs = refs[1:]
    (lamr_ref, lami_ref, bbr_ref, bbi_ref, cc_ref, dsk_ref, wglu_ref, lnw_ref, lnb_ref,
     o_ref, hro_ref, hio_ref,
     bur_ref, bui_ref, hsr_ref, hsi_ref, bdr_ref, bdi_ref, cdt_ref, slab_ref) = refs
    d = D_MODEL
    rows = bt * tc
    n_slab = d // LANES
    ti = pl.program_id(1)

    @pl.when(ti == 0)
    def _():
        if has_state:
            hsr_ref[...] = h0r_ref[...]
            hsi_ref[...] = h0i_ref[...]
        else:
            hsr_ref[...] = jnp.zeros(hsr_ref.shape, F32)
            hsi_ref[...] = jnp.zeros(hsi_ref.shape, F32)
        rep = MXU_TILE // S5_GROUP
        gi = lax.broadcasted_iota(jnp.int32, (MXU_TILE, S5_KT_COLS), 0) // S5_GROUP
        ci = lax.broadcasted_iota(jnp.int32, (MXU_TILE, S5_KT_COLS), 1) // S5_STATE
        in_mask = gi == ci
        out_mask = jnp.concatenate([in_mask, in_mask], axis=1)
        for kt in range(S5_KT):
            bdr_ref[kt] = jnp.where(in_mask, jnp.concatenate([bbr_ref[kt]] * rep, axis=0), 0.0).astype(BF16)
            bdi_ref[kt] = jnp.where(in_mask, jnp.concatenate([bbi_ref[kt]] * rep, axis=0), 0.0).astype(BF16)
            cdt_ref[kt] = jnp.where(out_mask, jnp.concatenate([cc_ref[kt]] * rep, axis=0), 0.0).astype(BF16)

    for s in range(n_slab):
        for b in range(bt):
            slab_ref[s, pl.ds(b, tc, stride=bt), :] = x_ref[b, :, s * LANES:(s + 1) * LANES]
    u = jnp.concatenate([slab_ref[s] for s in range(n_slab)], axis=1)
    ub = u.astype(BF16)
    for kt in range(S5_KT):
        uk = ub[:, kt * MXU_TILE:(kt + 1) * MXU_TILE]
        bur_ref[:, kt * S5_KT_COLS:(kt + 1) * S5_KT_COLS] = _dot(uk, bdr_ref[kt])
        bui_ref[:, kt * S5_KT_COLS:(kt + 1) * S5_KT_COLS] = _dot(uk, bdi_ref[kt])

    cols = 4 * VREG_ELEMS // bt
    for s in range(S5_COLS // cols):
        cs = slice(s * cols, (s + 1) * cols)
        lr = jnp.broadcast_to(lamr_ref[:, cs], (bt, cols))
        li = jnp.broadcast_to(lami_ref[:, cs], (bt, cols))
        hr = hsr_ref[:, cs]
        hi = hsi_ref[:, cs]
        for t in range(tc):
            rs = slice(t * bt, (t + 1) * bt)
            nr = lr * hr - li * hi + bur_ref[rs, cs]
            ni = lr * hi + li * hr + bui_ref[rs, cs]
            bur_ref[rs, cs] = nr
            bui_ref[rs, cs] = ni
            hr, hi = nr, ni
        hsr_ref[:, cs] = hr
        hsi_ref[:, cs] = hi
    hro_ref[...] = hsr_ref[...]
    hio_ref[...] = hsi_ref[...]

    ys = []
    for nt in range(S5_KT):
        hcat = jnp.concatenate([bur_ref[:, nt * S5_KT_COLS:(nt + 1) * S5_KT_COLS],
                                bui_ref[:, nt * S5_KT_COLS:(nt + 1) * S5_KT_COLS]], axis=1).astype(BF16)
        ys.append(lax.dot_general(hcat, cdt_ref[nt], NT_DIMS, preferred_element_type=F32))
    y = jnp.concatenate(ys, axis=1) + dsk_ref[...] * u
    y = y * (0.5 * (1.0 + jnp.tanh(math.sqrt(2.0 / math.pi) * (y + 0.044715 * (y * y * y)))))
    z = _dot(y.astype(BF16), wglu_ref[...])
    mix = z[:, :d] * _sigmoid(z[:, d:])
    xn = _layer_norm(ALPHA * u + mix, lnw_ref[...], lnb_ref[...])
    for s in range(n_slab):
        slab_ref[s] = xn[:, s * LANES:(s + 1) * LANES]
    for b in range(bt):
        for s in range(n_slab):
            o_ref[b, :, s * LANES:(s + 1) * LANES] = slab_ref[s, pl.ds(b, tc, stride=bt), :]


def _s5_call(x, h0, layer, hprev, lamr, lami, bbr, bbi, cc, dsk, wglu, lnw, lnb, mix_layer, *, bt, tc):
    bsz, t_len, d = x.shape
    n_layers = lamr.shape[0]
    has_state = h0 is not None
    rows = bt * tc
    xspec = pl.BlockSpec((bt, tc, d), lambda b, t: (b, t, 0))
    sspec = pl.BlockSpec((None, bt, S5_COLS), lambda b, t: (layer, b, 0))
    any_spec = pl.BlockSpec(memory_space=pl.ANY)
    in_specs = [xspec] + ([sspec, sspec] if has_state else []) + [
        _layer_block(lamr.shape, layer), _layer_block(lami.shape, layer), _layer_block(bbr.shape, layer),
        _layer_block(bbi.shape, layer), _layer_block(cc.shape, layer), _layer_block(dsk.shape, layer),
        _layer_block(wglu.shape, layer), _layer_block(lnw.shape, mix_layer), _layer_block(lnb.shape, mix_layer)]
    args = [x] + (list(h0) if has_state else []) + [lamr, lami, bbr, bbi, cc, dsk, wglu, lnw, lnb]
    aliases = {}
    if hprev is not None:
        aliases = {len(args): 1, len(args) + 1: 2}
        in_specs += [any_spec, any_spec]
        args += list(hprev)
    kern = functools.partial(_s5_kernel_entry, bt=bt, tc=tc, has_state=has_state, n_extra=len(aliases))
    st_shape = jax.ShapeDtypeStruct((n_layers, bsz, S5_COLS), F32)
    return pl.pallas_call(
        kern,
        grid=(bsz // bt, t_len // tc),
        in_specs=in_specs,
        out_specs=[xspec, sspec, sspec],
        out_shape=[jax.ShapeDtypeStruct(x.shape, F32), st_shape, st_shape],
        scratch_shapes=[pltpu.VMEM((rows, S5_COLS), F32), pltpu.VMEM((rows, S5_COLS), F32),
                        pltpu.VMEM((bt, S5_COLS), F32), pltpu.VMEM((bt, S5_COLS), F32),
                        pltpu.VMEM((S5_KT, MXU_TILE, S5_KT_COLS), BF16),
                        pltpu.VMEM((S5_KT, MXU_TILE, S5_KT_COLS), BF16),
                        pltpu.VMEM((S5_KT, MXU_TILE, 2 * S5_KT_COLS), BF16),
                        pltpu.VMEM((d // LANES, rows, LANES), F32)],
        input_output_aliases=aliases,
        compiler_params=pltpu.CompilerParams(dimension_semantics=("parallel", "arbitrary"),
                                             vmem_limit_bytes=VMEM_LIMIT_BYTES),
        name="s5_mixer",
    )(*args)


def _s5_kernel_entry(*refs, bt, tc, has_state, n_extra):
    n_in = 1 + (2 if has_state else 0) + 9
    _s5_kernel(*(refs[:n_in] + refs[n_in + n_extra:]), bt=bt, tc=tc, has_state=has_state)


def _s5_params(a_re, a_im, b_re, b_im, c_re, c_im, log_step):
    n_l = a_re.shape[0]
    delta = jnp.exp(log_step)[..., None]
    er = jnp.exp(a_re * delta)
    lr = er * jnp.cos(a_im * delta)
    li = er * jnp.sin(a_im * delta)
    den = a_re * a_re + a_im * a_im
    qr = ((lr - 1.0) * a_re + li * a_im) / den
    qi = (li * a_re - (lr - 1.0) * a_im) / den
    bbr = qr[..., None] * b_re - qi[..., None] * b_im
    bbi = qr[..., None] * b_im + qi[..., None] * b_re

    def compact_in(bb):
        w = bb.reshape(n_l, S5_KT, S5_GROUPS_PER_TILE, S5_STATE, S5_GROUP).transpose(0, 1, 4, 2, 3)
        return w.reshape(n_l, S5_KT, S5_GROUP, S5_KT_COLS)

    def compact_out(cm):
        w = cm.reshape(n_l, S5_KT, S5_GROUPS_PER_TILE, S5_GROUP, S5_STATE).transpose(0, 1, 3, 2, 4)
        return w.reshape(n_l, S5_KT, S5_GROUP, S5_KT_COLS)

    cc = jnp.concatenate([compact_out(c_re), -compact_out(c_im)], axis=-1)
    return (lr.reshape(n_l, 1, S5_COLS), li.reshape(n_l, 1, S5_COLS), compact_in(bbr), compact_in(bbi), cc)


def _hgrn_kernel(*refs, nseq, seg, c, seq):
    if seq:
        (x_ref, win_ref, lb_ref, gn_ref, wout_ref, lnw_ref, lnb_ref,
         o_ref, so_ref, st_ref, qd_ref, kd_ref, v_ref, e_ref, oi_ref) = refs
        s0_ref = None
    else:
        (x_ref, s0_ref, win_ref, lb_ref, gn_ref, wout_ref, lnw_ref, lnb_ref,
         o_ref, so_ref, qd_ref, kd_ref, v_ref, e_ref, oi_ref) = refs
        st_ref = None
    d = D_MODEL
    hd = HG_HEAD_DIM
    rows = nseq * seg
    nsub = seg // c
    lg = c.bit_length() - 1

    if seq:
        ti = pl.program_id(1)

        @pl.when(ti == 0)
        def _():
            st_ref[...] = jnp.zeros(st_ref.shape, F32)

    ri = lax.broadcasted_iota(jnp.int32, (seg, seg), 0)
    ci = lax.broadcasted_iota(jnp.int32, (seg, seg), 1)
    causal = jnp.logical_and((ri >> lg) == (ci >> lg), ci <= ri)
    tri = jnp.where(causal, 1.0, 0.0).astype(BF16)
    lb = lb_ref[...]
    ones_rows = jnp.ones((8, hd), BF16)

    for s in range(nseq):
        rs = slice(s * seg, (s + 1) * seg)
        x = x_ref[s] if seq else x_ref[...]
        proj = _dot(x.astype(BF16), win_ref[...])
        q = proj[:, :d]
        fz = proj[:, d:2 * d]
        v = proj[:, 2 * d:3 * d]
        g = proj[:, 3 * d:]
        f = lb + (1.0 - lb) * _sigmoid(fz)
        logf = jnp.log(f)
        kk = 1.0 - f
        qs = q * _sigmoid(q) * HG_QK_SCALE

        l_hi = logf.astype(BF16)
        l_lo = (logf - l_hi.astype(F32)).astype(BF16)
        b = _dot(tri, l_hi) + _dot(tri, l_lo)
        b3 = b.reshape(nsub, c, d)
        bm = b3[:, c // 2 - 1:c // 2, :]
        bl = b3[:, c - 1:c, :]
        d1 = b3 - bm
        qh3 = qs.reshape(nsub, c, d) * jnp.exp(d1)
        kh3 = kk.reshape(nsub, c, d) * jnp.exp(-d1)
        qd_ref[rs, :] = (qh3 * jnp.exp(bm)).reshape(seg, d)
        kd_ref[rs, :] = (kh3 * jnp.exp(bl - bm)).reshape(seg, d)
        qh = qh3.reshape(seg, d).astype(BF16)
        kh = kh3.reshape(seg, d).astype(BF16)
        v_ref[rs, :] = v
        e_ref[s * nsub:(s + 1) * nsub] = jnp.exp(bl)
        vb = v.astype(BF16)

        for h in range(HG_HEADS):
            cs = slice(h * hd, (h + 1) * hd)
            att = lax.dot_general(qh[:, cs], kh[:, cs], NT_DIMS, preferred_element_type=F32)
            att = jnp.where(causal, att, 0.0)
            oi_ref[rs, cs] = _dot(att.astype(BF16), vb[:, cs])

        def body(j, carry):
            r0 = s * seg + j * c
            for h in range(HG_HEADS):
                cs = slice(h * hd, (h + 1) * hd)
                qd = qd_ref[pl.ds(r0, c), cs].astype(BF16)
                kd = kd_ref[pl.ds(r0, c), cs].astype(BF16)
                vv = v_ref[pl.ds(r0, c), cs].astype(BF16)
                ej = e_ref[s * nsub + j, :, cs]
                if seq:
                    st = st_ref[s, h]
                    o_int = lax.dot_general(qd, st.astype(BF16), NT_DIMS, preferred_element_type=F32)
                    kv_t = lax.dot_general(vv, kd, TN_DIMS, preferred_element_type=F32)
                    st_ref[s, h] = st * ej + kv_t
                else:
                    st = s0_ref[j, h]
                    o_int = _dot(qd, st.astype(BF16))
                    e_col = lax.dot_general(_bf16_split_rows(ej), ones_rows, TN_DIMS,
                                            preferred_element_type=F32)
                    kv = lax.dot_general(kd, vv, TN_DIMS, preferred_element_type=F32)
                    so_ref[j, h] = st * e_col + kv
                oi_ref[pl.ds(r0, c), cs] = oi_ref[pl.ds(r0, c), cs] + o_int
            return carry

        for j in range(nsub):
            body(j, 0)

        o = oi_ref[rs, :]
        outs = []
        for h in range(HG_HEADS):
            oh = o[:, h * hd:(h + 1) * hd]
            outs.append(oh * lax.rsqrt(jnp.mean(oh * oh, axis=-1, keepdims=True) + RMS_EPS))
        y = jnp.concatenate(outs, axis=1) * gn_ref[...] * (g * _sigmoid(g))
        mix = _dot(y.astype(BF16), wout_ref[...])
        xn = _layer_norm(ALPHA * x + mix, lnw_ref[...], lnb_ref[...])
        if seq:
            o_ref[s] = xn
        else:
            o_ref[...] = xn

    if seq:
        @pl.when(ti == pl.num_programs(1) - 1)
        def _():
            for s in range(nseq):
                for h in range(HG_HEADS):
                    so_ref[s, h] = st_ref[s, h].T


def _hgrn_kernel_entry(*refs, n_in, n_extra, **kw):
    _hgrn_kernel(*(refs[:n_in] + refs[n_in + n_extra:]), **kw)


def _hgrn_scratch(rows, c):
    d = D_MODEL
    return [pltpu.VMEM((rows, d), F32), pltpu.VMEM((rows, d), F32), pltpu.VMEM((rows, d), F32),
            pltpu.VMEM((rows // c, 1, d), F32), pltpu.VMEM((rows, d), F32)]


def _hgrn_call(x, s0, layer, sprev, win, lb, gn, wout, lnw, lnb, mix_layer, *, tile, nseq=1):
    bsz, t_len, d = x.shape
    hd = HG_HEAD_DIM
    n_layers = win.shape[0]
    seq = s0 is None
    weights = [win, lb, gn, wout]
    w_specs = [_layer_block(w.shape, layer) for w in weights] + [
        _layer_block(lnw.shape, mix_layer), _layer_block(lnb.shape, mix_layer)]
    any_spec = pl.BlockSpec(memory_space=pl.ANY)
    st_shape = jax.ShapeDtypeStruct((n_layers, bsz, HG_HEADS, hd, hd), F32)
    if seq:
        seg, c = tile, HG_CHUNK
        rows = nseq * seg
        grid = (bsz // nseq, t_len // tile)
        xspec = pl.BlockSpec((nseq, tile, d), lambda b, t: (b, t, 0))
        sspec = pl.BlockSpec((None, nseq, HG_HEADS, hd, hd), lambda b, t: (layer, b, 0, 0, 0))
        in_specs, args = [xspec] + w_specs, [x] + weights + [lnw, lnb]
        scratch = [pltpu.VMEM((nseq, HG_HEADS, hd, hd), F32)] + _hgrn_scratch(rows, c)
        sem = ("parallel", "arbitrary")
        x_in, x_shape = x, x.shape
    else:
        rows, c, nseq = tile * t_len, t_len, 1
        seg = rows
        grid = (bsz // tile,)
        xspec = pl.BlockSpec((rows, d), lambda i: (i, 0))
        sspec = pl.BlockSpec((None, tile, HG_HEADS, hd, hd), lambda i: (layer, i, 0, 0, 0))
        x_in, x_shape = x.reshape(bsz * t_len, d), (bsz * t_len, d)
        in_specs, args = [xspec, sspec] + w_specs, [x_in, s0] + weights + [lnw, lnb]
        scratch = _hgrn_scratch(rows, c)
        sem = ("parallel",)
    n_in = len(args)
    aliases = {}
    if sprev is not None:
        aliases = {n_in: 1}
        in_specs, args = in_specs + [any_spec], args + [sprev]
    if seq:
        args[0] = x_in
    kern = functools.partial(_hgrn_kernel_entry, n_in=n_in, n_extra=len(aliases), nseq=nseq, seg=seg, c=c,
                             seq=seq)
    xo, so = pl.pallas_call(
        kern,
        grid=grid,
        in_specs=in_specs,
        out_specs=[xspec, sspec],
        out_shape=[jax.ShapeDtypeStruct(x_shape, F32), st_shape],
        scratch_shapes=scratch,
        input_output_aliases=aliases,
        compiler_params=pltpu.CompilerParams(dimension_semantics=sem, vmem_limit_bytes=VMEM_LIMIT_BYTES),
        name="hgrn_mixer_prompt" if seq else "hgrn_mixer_sample",
    )(*args)
    return xo.reshape(bsz, t_len, d), so


def _tile(n, target):
    t = min(n, target)
    while n % t:
        t //= 2
    return t


def kernel(x_prompt, x_sample, state_hgrn, state_s5_re, state_s5_im, p_prompt, p_sample, hg_w_in, hg_lower_bounds, hg_gnorm_w, hg_w_out, s5_a_re, s5_a_im, s5_b_re, s5_b_im, s5_c_re, s5_c_im, s5_d, s5_log_step, s5_w_glu, ln_mix_w, ln_mix_b, ffn_w_gate_up, ffn_w_down, ln_ffn_w, ln_ffn_b, ple_w_proj, ple_w_gate, ple_norm_w):
    bp, tp, d = x_prompt.shape
    bs, ts, _ = x_sample.shape
    depth = ln_mix_w.shape[0]
    ple = p_prompt.shape[-1]
    n_s5 = s5_a_re.shape[0]
    row3 = lambda a: a.reshape(a.shape[0], 1, -1)

    lb_soft = jax.nn.softmax(hg_lower_bounds.astype(F32), axis=0)
    lb_all = row3(jnp.cumsum(lb_soft, axis=0) - lb_soft[0])
    gn_all = row3(jnp.tile(hg_gnorm_w, (1, HG_HEADS)))
    hg_win, hg_wout = hg_w_in.astype(BF16), hg_w_out.astype(BF16)
    lamr, lami, bbr, bbi, cc = _s5_params(s5_a_re, s5_a_im, s5_b_re, s5_b_im, s5_c_re, s5_c_im, s5_log_step)
    dsk_all, wglu_all = row3(s5_d), s5_w_glu.astype(BF16)
    lnm_w, lnm_b = row3(ln_mix_w), row3(ln_mix_b)
    dense_w = (ffn_w_gate_up.astype(BF16), ffn_w_down.astype(BF16), row3(ln_ffn_w), row3(ln_ffn_b),
               ple_w_proj.astype(BF16), ple_w_gate.astype(BF16), row3(ple_norm_w))
    pp = p_prompt.reshape(depth, bp * tp, ple)
    ps = p_sample.reshape(depth, bs * ts, ple)
    h0_s = (state_s5_re.reshape(n_s5, bs, S5_COLS), state_s5_im.reshape(n_s5, bs, S5_COLS))

    xp, xs = x_prompt, x_sample
    hg_p = hg_s = s5_p = s5_s = None
    for i in range(depth):
        j = i // 2
        if i % 2 == 0:
            hg_args = (hg_win, lb_all, gn_all, hg_wout, lnm_w, lnm_b, i)
            xp, hg_p = _hgrn_call(xp, None, j, hg_p, *hg_args, tile=_tile(tp, 256), nseq=_tile(bp, 2))
            xs, hg_s = _hgrn_call(xs, state_hgrn, j, hg_s, *hg_args, tile=_tile(bs, 16))
        else:
            s5_args = (lamr, lami, bbr, bbi, cc, dsk_all, wglu_all, lnm_w, lnm_b, i)
            xp, *s5_p = _s5_call(xp, None, j, s5_p, *s5_args, bt=bp, tc=_tile(tp, 256 // bp))
            xs, *s5_s = _s5_call(xs, h0_s, j, s5_s, *s5_args, bt=_tile(bs, 256 // ts), tc=ts)
        xp = _dense_call(xp.reshape(bp * tp, d), pp, i, *dense_w, tm=_tile(bp * tp, 512)).reshape(bp, tp, d)
        xs = _dense_call(xs.reshape(bs * ts, d), ps, i, *dense_w, tm=_tile(bs * ts, 512)).reshape(bs, ts, d)
    s5_shape = lambda a, n: a.reshape(n_s5, n, S5_GROUPS, S5_STATE)
    return (xp, xs, hg_p, s5_shape(s5_p[0], bp), s5_shape(s5_p[1], bp),
            hg_s, s5_shape(s5_s[0], bs), s5_shape(s5_s[1], bs))
```

```python
import functools
import math

import jax
import jax.numpy as jnp
from jax import lax
from jax.experimental import pallas as pl
from jax.experimental.pallas import tpu as pltpu

F32 = jnp.float32
BF16 = jnp.bfloat16

D_MODEL = 1024
HG_HEAD_DIM = 128
HG_HEADS = D_MODEL // HG_HEAD_DIM
HG_QK_SCALE = HG_HEAD_DIM ** -0.5
HG_CHUNK = 16
S5_GROUP = 16
S5_GROUPS = D_MODEL // S5_GROUP
S5_STATE = 64
S5_COLS = S5_GROUPS * S5_STATE
DEPTH = 4
ALPHA = (2 * DEPTH) ** 0.25
LN_EPS = 1e-5
RMS_EPS = 1e-6

V7X_VMEM_BYTES = 64 * 1024 * 1024
VMEM_LIMIT_BYTES = V7X_VMEM_BYTES - 8 * 1024 * 1024
MXU_TILE = 256
LANES = 128
VREG_ELEMS = 8 * LANES

S5_GROUPS_PER_TILE = MXU_TILE // S5_GROUP
S5_KT = S5_GROUPS // S5_GROUPS_PER_TILE
S5_KT_COLS = S5_GROUPS_PER_TILE * S5_STATE

NT_DIMS = (((1,), (1,)), ((), ()))
TN_DIMS = (((0,), (0,)), ((), ()))


def _dot(a, b):
    return jnp.dot(a, b, preferred_element_type=F32)


def _sigmoid(x):
    return 0.5 * jnp.tanh(0.5 * x) + 0.5


def _layer_norm(x, w, b):
    mu = jnp.mean(x, axis=-1, keepdims=True)
    xc = x - mu
    var = jnp.mean(xc * xc, axis=-1, keepdims=True)
    return xc * lax.rsqrt(var + LN_EPS) * w + b


def _rms_norm(x, w):
    return x * lax.rsqrt(jnp.mean(x * x, axis=-1, keepdims=True) + RMS_EPS) * w


def _bf16_split_rows(e):
    hi = e.astype(BF16).astype(F32)
    r1 = e - hi
    mid = r1.astype(BF16).astype(F32)
    lo = (r1 - mid).astype(BF16).astype(F32)
    row = lax.broadcasted_iota(jnp.int32, (8, e.shape[1]), 0)
    return jnp.where(row == 0, hi, jnp.where(row == 1, mid, jnp.where(row == 2, lo, 0.0))).astype(BF16)


def _resident(shape):
    nd = len(shape)
    return pl.BlockSpec(shape, lambda *_: (0,) * nd, pipeline_mode=pl.Buffered(1))


def _layer_block(shape, layer):
    nd = len(shape)
    return pl.BlockSpec((None,) + tuple(shape[1:]), lambda *_: (layer,) + (0,) * (nd - 1),
                        pipeline_mode=pl.Buffered(1))


def _dense_kernel(x_ref, p_ref, wgu_ref, wd_ref, lnw_ref, lnb_ref, wproj_ref, wgate_ref, nw_ref, o_ref,
                  *, d_ff, ff_chunk):
    x = x_ref[...]
    xb = x.astype(BF16)
    acc = jnp.zeros(x.shape, F32)
    for j in range(d_ff // ff_chunk):
        lo = j * ff_chunk
        g = _dot(xb, wgu_ref[:, lo:lo + ff_chunk])
        u = _dot(xb, wgu_ref[:, d_ff + lo:d_ff + lo + ff_chunk])
        h = (g * _sigmoid(g)) * u
        acc = acc + _dot(h.astype(BF16), wd_ref[lo:lo + ff_chunk, :])
    x2 = _layer_norm(ALPHA * x + acc, lnw_ref[...], lnb_ref[...])
    e = _dot(p_ref[...].astype(BF16), wproj_ref[...]) * _sigmoid(_dot(x2.astype(BF16), wgate_ref[...]))
    o_ref[...] = x2 + _rms_norm(e, nw_ref[...])


def _dense_call(x, p_all, layer, wgu, wd, lnw, lnb, wproj, wgate, nw, *, tm):
    n, d = x.shape
    d_ff = wd.shape[1]
    ple = p_all.shape[-1]
    row = lambda i: (i, 0)
    kern = functools.partial(_dense_kernel, d_ff=d_ff, ff_chunk=MXU_TILE)
    return pl.pallas_call(
        kern,
        grid=(n // tm,),
        in_specs=[pl.BlockSpec((tm, d), row), pl.BlockSpec((None, tm, ple), lambda i: (layer, i, 0)),
                  _layer_block(wgu.shape, layer), _layer_block(wd.shape, layer),
                  _layer_block(lnw.shape, layer), _layer_block(lnb.shape, layer),
                  _layer_block(wproj.shape, layer), _layer_block(wgate.shape, layer),
                  _layer_block(nw.shape, layer)],
        out_specs=pl.BlockSpec((tm, d), row),
        out_shape=jax.ShapeDtypeStruct((n, d), F32),
        compiler_params=pltpu.CompilerParams(dimension_semantics=("parallel",),
                                             vmem_limit_bytes=VMEM_LIMIT_BYTES),
        name="dense_ffn_ple",
    )(x, p_all, wgu, wd, lnw, lnb, wproj, wgate, nw)


def _s5_kernel(*refs, bt, tc, nsub, has_state):
    if has_state:
        x_ref, h0r_ref, h0i_ref = refs[:3]
        refs = refs[3:]
    else:
        x_ref = refs[0]
        refs = refs[1:]
    (lamr_ref, lami_ref, bbr_ref, bbi_ref, cc_ref, dsk_ref, wglu_ref, lnw_ref, lnb_ref,
     o_ref, hro_ref, hio_ref,
     bur_ref, bui_ref, hsr_ref, hsi_ref, bdr_ref, bdi_ref, cdt_ref, slab_ref) = refs
    d = D_MODEL
    n_slab = d // LANES
    ti = pl.program_id(1)

    @pl.when(ti == 0)
    def _():
        if has_state:
            hsr_ref[...] = h0r_ref[...]
            hsi_ref[...] = h0i_ref[...]
        else:
            hsr_ref[...] = jnp.zeros(hsr_ref.shape, F32)
            hsi_ref[...] = jnp.zeros(hsi_ref.shape, F32)
        rep = MXU_TILE // S5_GROUP
        gi = lax.broadcasted_iota(jnp.int32, (MXU_TILE, S5_KT_COLS), 0) // S5_GROUP
        ci = lax.broadcasted_iota(jnp.int32, (MXU_TILE, S5_KT_COLS), 1) // S5_STATE
        in_mask = gi == ci
        out_mask = jnp.concatenate([in_mask, in_mask], axis=1)
        for kt in range(S5_KT):
            bdr_ref[kt] = jnp.where(in_mask, jnp.concatenate([bbr_ref[kt]] * rep, axis=0), 0.0).astype(BF16)
            bdi_ref[kt] = jnp.where(in_mask, jnp.concatenate([bbi_ref[kt]] * rep, axis=0), 0.0).astype(BF16)
            cdt_ref[kt] = jnp.where(out_mask, jnp.concatenate([cc_ref[kt]] * rep, axis=0), 0.0).astype(BF16)

    ts = tc // nsub

    def load_rows(k):
        return jnp.concatenate([slab_ref[k, s] for s in range(n_slab)], axis=1)

    def project_in(k):
        for s in range(n_slab):
            for b in range(bt):
                slab_ref[k, s, pl.ds(b, ts, stride=bt), :] = x_ref[b, k * ts:(k + 1) * ts, s * LANES:(s + 1) * LANES]
        ub = load_rows(k).astype(BF16)
        for kt in range(S5_KT):
            uk = ub[:, kt * MXU_TILE:(kt + 1) * MXU_TILE]
            bur_ref[k, :, kt * S5_KT_COLS:(kt + 1) * S5_KT_COLS] = _dot(uk, bdr_ref[kt])
            bui_ref[k, :, kt * S5_KT_COLS:(kt + 1) * S5_KT_COLS] = _dot(uk, bdi_ref[kt])

    def recurrence(k):
        cols = 4 * VREG_ELEMS // bt
        for s in range(S5_COLS // cols):
            cs = slice(s * cols, (s + 1) * cols)
            lr = jnp.broadcast_to(lamr_ref[:, cs], (bt, cols))
            li = jnp.broadcast_to(lami_ref[:, cs], (bt, cols))
            hr = hsr_ref[:, cs]
            hi = hsi_ref[:, cs]
            for t in range(ts):
                rs = slice(t * bt, (t + 1) * bt)
                nr = lr * hr - li * hi + bur_ref[k, rs, cs]
                ni = lr * hi + li * hr + bui_ref[k, rs, cs]
                bur_ref[k, rs, cs] = nr
                bui_ref[k, rs, cs] = ni
                hr, hi = nr, ni
            hsr_ref[:, cs] = hr
            hsi_ref[:, cs] = hi

    def project_out(k):
        u = load_rows(k)
        ys = []
        for nt in range(S5_KT):
            hcat = jnp.concatenate([bur_ref[k, :, nt * S5_KT_COLS:(nt + 1) * S5_KT_COLS],
                                    bui_ref[k, :, nt * S5_KT_COLS:(nt + 1) * S5_KT_COLS]], axis=1).astype(BF16)
            ys.append(lax.dot_general(hcat, cdt_ref[nt], NT_DIMS, preferred_element_type=F32))
        y = jnp.concatenate(ys, axis=1) + dsk_ref[...] * u
        y = y * (0.5 * (1.0 + jnp.tanh(math.sqrt(2.0 / math.pi) * (y + 0.044715 * (y * y * y)))))
        z = _dot(y.astype(BF16), wglu_ref[...])
        mix = z[:, :d] * _sigmoid(z[:, d:])
        xn = _layer_norm(ALPHA * u + mix, lnw_ref[...], lnb_ref[...])
        for s in range(n_slab):
            slab_ref[k, s] = xn[:, s * LANES:(s + 1) * LANES]
        for b in range(bt):
            for s in range(n_slab):
                o_ref[b, k * ts:(k + 1) * ts, s * LANES:(s + 1) * LANES] = slab_ref[k, s, pl.ds(b, ts, stride=bt), :]

    for k in range(nsub):
        project_in(k)
    for k in range(nsub):
        recurrence(k)
    hro_ref[...] = hsr_ref[...]
    hio_ref[...] = hsi_ref[...]
    for k in range(nsub):
        project_out(k)


def _s5_call(x, h0, layer, hprev, lamr, lami, bbr, bbi, cc, dsk, wglu, lnw, lnb, mix_layer, *, bt, tc, nsub):
    bsz, t_len, d = x.shape
    n_layers = lamr.shape[0]
    has_state = h0 is not None
    rows = bt * tc // nsub
    xspec = pl.BlockSpec((bt, tc, d), lambda b, t: (b, t, 0))
    sspec = pl.BlockSpec((None, bt, S5_COLS), lambda b, t: (layer, b, 0))
    any_spec = pl.BlockSpec(memory_space=pl.ANY)
    in_specs = [xspec] + ([sspec, sspec] if has_state else []) + [
        _layer_block(lamr.shape, layer), _layer_block(lami.shape, layer), _layer_block(bbr.shape, layer),
        _layer_block(bbi.shape, layer), _layer_block(cc.shape, layer), _layer_block(dsk.shape, layer),
        _layer_block(wglu.shape, layer), _layer_block(lnw.shape, mix_layer), _layer_block(lnb.shape, mix_layer)]
    args = [x] + (list(h0) if has_state else []) + [lamr, lami, bbr, bbi, cc, dsk, wglu, lnw, lnb]
    aliases = {}
    if hprev is not None:
        aliases = {len(args): 1, len(args) + 1: 2}
        in_specs += [any_spec, any_spec]
        args += list(hprev)
    kern = functools.partial(_s5_kernel_entry, bt=bt, tc=tc, nsub=nsub, has_state=has_state,
                             n_extra=len(aliases))
    st_shape = jax.ShapeDtypeStruct((n_layers, bsz, S5_COLS), F32)
    return pl.pallas_call(
        kern,
        grid=(bsz // bt, t_len // tc),
        in_specs=in_specs,
        out_specs=[xspec, sspec, sspec],
        out_shape=[jax.ShapeDtypeStruct(x.shape, F32), st_shape, st_shape],
        scratch_shapes=[pltpu.VMEM((nsub, rows, S5_COLS), F32), pltpu.VMEM((nsub, rows, S5_COLS), F32),
                        pltpu.VMEM((bt, S5_COLS), F32), pltpu.VMEM((bt, S5_COLS), F32),
                        pltpu.VMEM((S5_KT, MXU_TILE, S5_KT_COLS), BF16),
                        pltpu.VMEM((S5_KT, MXU_TILE, S5_KT_COLS), BF16),
                        pltpu.VMEM((S5_KT, MXU_TILE, 2 * S5_KT_COLS), BF16),
                        pltpu.VMEM((nsub, d // LANES, rows, LANES), F32)],
        input_output_aliases=aliases,
        compiler_params=pltpu.CompilerParams(dimension_semantics=("parallel", "arbitrary"),
                                             vmem_limit_bytes=VMEM_LIMIT_BYTES),
        name="s5_mixer",
    )(*args)


def _s5_kernel_entry(*refs, has_state, n_extra, **kw):
    n_in = 1 + (2 if has_state else 0) + 9
    _s5_kernel(*(refs[:n_in] + refs[n_in + n_extra:]), has_state=has_state, **kw)


def _s5_params(a_re, a_im, b_re, b_im, c_re, c_im, log_step):
    n_l = a_re.shape[0]
    delta = jnp.exp(log_step)[..., None]
    er = jnp.exp(a_re * delta)
    lr = er * jnp.cos(a_im * delta)
    li = er * jnp.sin(a_im * delta)
    den = a_re * a_re + a_im * a_im
    qr = ((lr - 1.0) * a_re + li * a_im) / den
    qi = (li * a_re - (lr - 1.0) * a_im) / den
    bbr = qr[..., None] * b_re - qi[..., None] * b_im
    bbi = qr[..., None] * b_im + qi[..., None] * b_re

    def compact_in(bb):
        w = bb.reshape(n_l, S5_KT, S5_GROUPS_PER_TILE, S5_STATE, S5_GROUP).transpose(0, 1, 4, 2, 3)
        return w.reshape(n_l, S5_KT, S5_GROUP, S5_KT_COLS)

    def compact_out(cm):
        w = cm.reshape(n_l, S5_KT, S5_GROUPS_PER_TILE, S5_GROUP, S5_STATE).transpose(0, 1, 3, 2, 4)
        return w.reshape(n_l, S5_KT, S5_GROUP, S5_KT_COLS)

    cc = jnp.concatenate([compact_out(c_re), -compact_out(c_im)], axis=-1)
    return (lr.reshape(n_l, 1, S5_COLS), li.reshape(n_l, 1, S5_COLS), compact_in(bbr), compact_in(bbi), cc)


def _hgrn_kernel(*refs, nseq, seg, c, seq):
    if seq:
        (x_ref, win_ref, lb_ref, gn_ref, wout_ref, lnw_ref, lnb_ref,
         o_ref, so_ref, st_ref, qd_ref, kd_ref, v_ref, e_ref, oi_ref) = refs
        s0_ref = None
    else:
        (x_ref, s0_ref, win_ref, lb_ref, gn_ref, wout_ref, lnw_ref, lnb_ref,
         o_ref, so_ref, qd_ref, kd_ref, v_ref, e_ref, oi_ref) = refs
        st_ref = None
    d = D_MODEL
    hd = HG_HEAD_DIM
    rows = nseq * seg
    nsub = seg // c
    lg = c.bit_length() - 1

    if seq:
        ti = pl.program_id(1)

        @pl.when(ti == 0)
        def _():
            st_ref[...] = jnp.zeros(st_ref.shape, F32)

    ri = lax.broadcasted_iota(jnp.int32, (seg, seg), 0)
    ci = lax.broadcasted_iota(jnp.int32, (seg, seg), 1)
    causal = jnp.logical_and((ri >> lg) == (ci >> lg), ci <= ri)
    tri = jnp.where(causal, 1.0, 0.0).astype(BF16)
    pair = seq
    cl = 2 * c if pair else c
    nloop = seg // cl
    cross = jnp.logical_and(((ri >> lg) & 1) == 1, (ci >> lg) == (ri >> lg) - 1)
    lb = lb_ref[...]
    ones_rows = jnp.ones((8, hd), BF16)

    def project(s):
        x = x_ref[s * seg:(s + 1) * seg, :]
        return x, _dot(x.astype(BF16), win_ref[...])

    def gates_and_local_attention(s, proj):
        rs = slice(s * seg, (s + 1) * seg)
        q = proj[:, :d]
        fz = proj[:, d:2 * d]
        v = proj[:, 2 * d:3 * d]
        f = lb + (1.0 - lb) * _sigmoid(fz)
        logf = jnp.log(f)
        kk = 1.0 - f
        qs = q * _sigmoid(q) * HG_QK_SCALE

        l_hi = logf.astype(BF16)
        l_lo = (logf - l_hi.astype(F32)).astype(BF16)
        b = _dot(tri, l_hi) + _dot(tri, l_lo)
        b3 = b.reshape(nsub, c, d)
        bm = b3[:, c // 2 - 1:c // 2, :]
        bl = b3[:, c - 1:c, :]
        d1 = b3 - bm
        qh3 = qs.reshape(nsub, c, d) * jnp.exp(d1)
        kh3 = kk.reshape(nsub, c, d) * jnp.exp(-d1)
        qd3 = qh3 * jnp.exp(bm)
        kd3 = kh3 * jnp.exp(bl - bm)
        e3 = jnp.exp(bl)
        qh = qh3.reshape(seg, d).astype(BF16)
        kh = kh3.reshape(seg, d).astype(BF16)
        vb = v.astype(BF16)
        v_ref[rs, :] = v.astype(v_ref.dtype)
        if pair:
            qd4 = qd3.reshape(nloop, 2, c, d)
            kd4 = kd3.reshape(nloop, 2, c, d)
            e4 = e3.reshape(nloop, 2, 1, d)
            qd_loop = jnp.concatenate([qd4[:, 0:1], qd4[:, 1:2] * e4[:, 0:1]], axis=1)
            kd_loop = jnp.concatenate([kd4[:, 0:1] * e4[:, 1:2], kd4[:, 1:2]], axis=1)
            qd_ref[rs, :] = qd_loop.reshape(seg, d).astype(qd_ref.dtype)
            kd_ref[rs, :] = kd_loop.reshape(seg, d).astype(kd_ref.dtype)
            e_ref[s * nloop:(s + 1) * nloop] = e4[:, 0] * e4[:, 1]
            qdb = qd3.reshape(seg, d).astype(BF16)
            kdb = kd3.reshape(seg, d).astype(BF16)
        else:
            qd_ref[rs, :] = qd3.reshape(seg, d).astype(qd_ref.dtype)
            kd_ref[rs, :] = kd3.reshape(seg, d).astype(kd_ref.dtype)
            e_ref[s * nloop:(s + 1) * nloop] = e3

        for h in range(HG_HEADS):
            cs = slice(h * hd, (h + 1) * hd)
            att = lax.dot_general(qh[:, cs], kh[:, cs], NT_DIMS, preferred_element_type=F32)
            att = jnp.where(causal, att, 0.0)
            if pair:
                att1 = lax.dot_general(qdb[:, cs], kdb[:, cs], NT_DIMS, preferred_element_type=F32)
                att = att + jnp.where(cross, att1, 0.0)
            oi_ref[rs, cs] = _dot(att.astype(BF16), vb[:, cs])

    def state_recurrence(s):
        for j in range(nloop):
            r0 = s * seg + j * cl
            for h in range(HG_HEADS):
                cs = slice(h * hd, (h + 1) * hd)
                qd = qd_ref[pl.ds(r0, cl), cs].astype(BF16)
                kd = kd_ref[pl.ds(r0, cl), cs].astype(BF16)
                vv = v_ref[pl.ds(r0, cl), cs].astype(BF16)
                ej = e_ref[s * nloop + j, :, cs]
                if seq:
                    st = st_ref[h]
                    o_int = lax.dot_general(qd, st.astype(BF16), NT_DIMS, preferred_element_type=F32)
                    kv_t = lax.dot_general(vv, kd, TN_DIMS, preferred_element_type=F32)
                    st_ref[h] = st * ej + kv_t
                else:
                    st = s0_ref[j, h]
                    o_int = _dot(qd, st.astype(BF16))
                    e_col = lax.dot_general(_bf16_split_rows(ej), ones_rows, TN_DIMS,
                                            preferred_element_type=F32)
                    kv = lax.dot_general(kd, vv, TN_DIMS, preferred_element_type=F32)
                    so_ref[j, h] = st * e_col + kv
                oi_ref[pl.ds(r0, cl), cs] = oi_ref[pl.ds(r0, cl), cs] + o_int

    def output(s, x, proj):
        g = proj[:, 3 * d:]
        o = oi_ref[s * seg:(s + 1) * seg, :]
        outs = []
        for h in range(HG_HEADS):
            oh = o[:, h * hd:(h + 1) * hd]
            outs.append(oh * lax.rsqrt(jnp.mean(oh * oh, axis=-1, keepdims=True) + RMS_EPS))
        y = jnp.concatenate(outs, axis=1) * gn_ref[...] * (g * _sigmoid(g))
        mix = _dot(y.astype(BF16), wout_ref[...])
        o_ref[s * seg:(s + 1) * seg, :] = _layer_norm(ALPHA * x + mix, lnw_ref[...], lnb_ref[...])

    projected = [project(s) for s in range(nseq)]
    for s in range(nseq):
        gates_and_local_attention(s, projected[s][1])
    for s in range(nseq):
        state_recurrence(s)
    for s in range(nseq):
        output(s, *projected[s])

    if seq:
        @pl.when(ti == pl.num_programs(1) - 1)
        def _():
            for h in range(HG_HEADS):
                so_ref[h] = st_ref[h].T


def _hgrn_kernel_entry(*refs, n_in, n_extra, **kw):
    _hgrn_kernel(*(refs[:n_in] + refs[n_in + n_extra:]), **kw)


def _hgrn_scratch(rows, c, operand_dtype):
    d = D_MODEL
    return [pltpu.VMEM((rows, d), operand_dtype), pltpu.VMEM((rows, d), operand_dtype),
            pltpu.VMEM((rows, d), operand_dtype), pltpu.VMEM((rows // c, 1, d), F32),
            pltpu.VMEM((rows, d), F32)]


def _hgrn_call(x, s0, layer, sprev, win, lb, gn, wout, lnw, lnb, mix_layer, *, tile, nseq=1):
    bsz, t_len, d = x.shape
    hd = HG_HEAD_DIM
    n_layers = win.shape[0]
    seq = s0 is None
    weights = [win, lb, gn, wout]
    w_specs = [_layer_block(w.shape, layer) for w in weights] + [
        _layer_block(lnw.shape, mix_layer), _layer_block(lnb.shape, mix_layer)]
    any_spec = pl.BlockSpec(memory_space=pl.ANY)
    st_shape = jax.ShapeDtypeStruct((n_layers, bsz, HG_HEADS, hd, hd), F32)
    if seq:
        seg, c = tile, HG_CHUNK
        rows = nseq * seg
        grid = (bsz, t_len // rows)
        xspec = pl.BlockSpec((None, rows, d), lambda b, t: (b, t, 0))
        sspec = pl.BlockSpec((None, None, HG_HEADS, hd, hd), lambda b, t: (layer, b, 0, 0, 0))
        in_specs, args = [xspec] + w_specs, [x] + weights + [lnw, lnb]
        scratch = [pltpu.VMEM((HG_HEADS, hd, hd), F32)] + _hgrn_scratch(rows, c, BF16)
        sem = ("parallel", "arbitrary")
        x_in, x_shape = x, x.shape
    else:
        rows, c, nseq = tile * t_len, t_len, 1
        seg = rows
        grid = (bsz // tile,)
        xspec = pl.BlockSpec((rows, d), lambda i: (i, 0))
        sspec = pl.BlockSpec((None, tile, HG_HEADS, hd, hd), lambda i: (layer, i, 0, 0, 0))
        x_in, x_shape = x.reshape(bsz * t_len, d), (bsz * t_len, d)
        in_specs, args = [xspec, sspec] + w_specs, [x_in, s0] + weights + [lnw, lnb]
        scratch = _hgrn_scratch(rows, c, F32)
        sem = ("parallel",)
    n_in = len(args)
    aliases = {}
    if sprev is not None:
        aliases = {n_in: 1}
        in_specs, args = in_specs + [any_spec], args + [sprev]
    if seq:
        args[0] = x_in
    kern = functools.partial(_hgrn_kernel_entry, n_in=n_in, n_extra=len(aliases), nseq=nseq, seg=seg, c=c,
                             seq=seq)
    xo, so = pl.pallas_call(
        kern,
        grid=grid,
        in_specs=in_specs,
        out_specs=[xspec, sspec],
        out_shape=[jax.ShapeDtypeStruct(x_shape, F32), st_shape],
        scratch_shapes=scratch,
        input_output_aliases=aliases,
        compiler_params=pltpu.CompilerParams(dimension_semantics=sem, vmem_limit_bytes=VMEM_LIMIT_BYTES),
        name="hgrn_mixer_prompt" if seq else "hgrn_mixer_sample",
    )(*args)
    return xo.reshape(bsz, t_len, d), so


def _tile(n, target):
    t = min(n, target)
    while n % t:
        t //= 2
    return t


def kernel(x_prompt, x_sample, state_hgrn, state_s5_re, state_s5_im, p_prompt, p_sample, hg_w_in, hg_lower_bounds, hg_gnorm_w, hg_w_out, s5_a_re, s5_a_im, s5_b_re, s5_b_im, s5_c_re, s5_c_im, s5_d, s5_log_step, s5_w_glu, ln_mix_w, ln_mix_b, ffn_w_gate_up, ffn_w_down, ln_ffn_w, ln_ffn_b, ple_w_proj, ple_w_gate, ple_norm_w):
    bp, tp, d = x_prompt.shape
    bs, ts, _ = x_sample.shape
    depth = ln_mix_w.shape[0]
    ple = p_prompt.shape[-1]
    n_s5 = s5_a_re.shape[0]
    row3 = lambda a: a.reshape(a.shape[0], 1, -1)

    lb_soft = jax.nn.softmax(hg_lower_bounds.astype(F32), axis=0)
    lb_all = row3(jnp.cumsum(lb_soft, axis=0) - lb_soft[0])
    gn_all = row3(jnp.tile(hg_gnorm_w, (1, HG_HEADS)))
    hg_win, hg_wout = hg_w_in.astype(BF16), hg_w_out.astype(BF16)
    lamr, lami, bbr, bbi, cc = _s5_params(s5_a_re, s5_a_im, s5_b_re, s5_b_im, s5_c_re, s5_c_im, s5_log_step)
    dsk_all, wglu_all = row3(s5_d), s5_w_glu.astype(BF16)
    lnm_w, lnm_b = row3(ln_mix_w), row3(ln_mix_b)
    dense_w = (ffn_w_gate_up.astype(BF16), ffn_w_down.astype(BF16), row3(ln_ffn_w), row3(ln_ffn_b),
               ple_w_proj.astype(BF16), ple_w_gate.astype(BF16), row3(ple_norm_w))
    pp = p_prompt.reshape(depth, bp * tp, ple)
    ps = p_sample.reshape(depth, bs * ts, ple)
    h0_s = (state_s5_re.reshape(n_s5, bs, S5_COLS), state_s5_im.reshape(n_s5, bs, S5_COLS))

    xp, xs = x_prompt, x_sample
    hg_p = hg_s = s5_p = s5_s = None
    for i in range(depth):
        j = i // 2
        if i % 2 == 0:
            hg_args = (hg_win, lb_all, gn_all, hg_wout, lnm_w, lnm_b, i)
            xp, hg_p = _hgrn_call(xp, None, j, hg_p, *hg_args, tile=_tile(tp, 256), nseq=2)
            xs, hg_s = _hgrn_call(xs, state_hgrn, j, hg_s, *hg_args, tile=_tile(bs, 16))
        else:
            s5_args = (lamr, lami, bbr, bbi, cc, dsk_all, wglu_all, lnm_w, lnm_b, i)
            xp, *s5_p = _s5_call(xp, None, j, s5_p, *s5_args, bt=bp, tc=_tile(tp, 512 // bp), nsub=2)
            xs, *s5_s = _s5_call(xs, h0_s, j, s5_s, *s5_args, bt=_tile(bs, 256 // ts), tc=ts, nsub=1)
        xp = _dense_call(xp.reshape(bp * tp, d), pp, i, *dense_w, tm=_tile(bp * tp, 512)).reshape(bp, tp, d)
        xs = _dense_call(xs.reshape(bs * ts, d), ps, i, *dense_w, tm=_tile(bs * ts, 512)).reshape(bs, ts, d)
    s5_shape = lambda a, n: a.reshape(n_s5, n, S5_GROUPS, S5_STATE)
    return (xp, xs, hg_p, s5_shape(s5_p[0], bp), s5_shape(s5_p[1], bp),
            hg_s, s5_shape(s5_s[0], bs), s5_shape(s5_s[1], bs))
```

```python
import functools
import math

import jax
import jax.numpy as jnp
from jax import lax
from jax.experimental import pallas as pl
from jax.experimental.pallas import tpu as pltpu

F32 = jnp.float32
BF16 = jnp.bfloat16

D_MODEL = 1024
HG_HEAD_DIM = 128
HG_HEADS = D_MODEL // HG_HEAD_DIM
HG_QK_SCALE = HG_HEAD_DIM ** -0.5
HG_CHUNK = 16
S5_GROUP = 16
S5_GROUPS = D_MODEL // S5_GROUP
S5_STATE = 64
S5_COLS = S5_GROUPS * S5_STATE
DEPTH = 4
ALPHA = (2 * DEPTH) ** 0.25
LN_EPS = 1e-5
RMS_EPS = 1e-6

V7X_VMEM_BYTES = 64 * 1024 * 1024
VMEM_LIMIT_BYTES = V7X_VMEM_BYTES - 8 * 1024 * 1024
MXU_TILE = 256
LANES = 128
VREG_ELEMS = 8 * LANES

S5_GROUPS_PER_TILE = MXU_TILE // S5_GROUP
S5_KT = S5_GROUPS // S5_GROUPS_PER_TILE
S5_KT_COLS = S5_GROUPS_PER_TILE * S5_STATE

NT_DIMS = (((1,), (1,)), ((), ()))
TN_DIMS = (((0,), (0,)), ((), ()))


def _dot(a, b):
    return jnp.dot(a, b, preferred_element_type=F32)


def _sigmoid(x):
    return 0.5 * jnp.tanh(0.5 * x) + 0.5


def _sigmoid_rel(x):
    return 1.0 / (1.0 + jnp.exp(-x))


def _layer_norm(x, w, b):
    mu = jnp.mean(x, axis=-1, keepdims=True)
    xc = x - mu
    var = jnp.mean(xc * xc, axis=-1, keepdims=True)
    return xc * lax.rsqrt(var + LN_EPS) * w + b


def _rms_norm(x, w):
    return x * lax.rsqrt(jnp.mean(x * x, axis=-1, keepdims=True) + RMS_EPS) * w


def _bf16_split_rows(e):
    hi = e.astype(BF16).astype(F32)
    r1 = e - hi
    mid = r1.astype(BF16).astype(F32)
    lo = (r1 - mid).astype(BF16).astype(F32)
    row = lax.broadcasted_iota(jnp.int32, (8, e.shape[1]), 0)
    return jnp.where(row == 0, hi, jnp.where(row == 1, mid, jnp.where(row == 2, lo, 0.0))).astype(BF16)


def _resident(shape):
    nd = len(shape)
    return pl.BlockSpec(shape, lambda *_: (0,) * nd, pipeline_mode=pl.Buffered(1))


def _layer_block(shape, layer):
    nd = len(shape)
    return pl.BlockSpec((None,) + tuple(shape[1:]), lambda *_: (layer,) + (0,) * (nd - 1),
                        pipeline_mode=pl.Buffered(1))


def _dense_kernel(x_ref, p_ref, wgu_ref, wd_ref, lnw_ref, lnb_ref, wproj_ref, wgate_ref, nw_ref, o_ref,
                  *, d_ff, ff_chunk):
    x = x_ref[...]
    xb = x.astype(BF16)
    acc = jnp.zeros(x.shape, F32)
    for j in range(d_ff // ff_chunk):
        lo = j * ff_chunk
        g = _dot(xb, wgu_ref[:, lo:lo + ff_chunk])
        u = _dot(xb, wgu_ref[:, d_ff + lo:d_ff + lo + ff_chunk])
        h = (g * _sigmoid(g)) * u
        acc = acc + _dot(h.astype(BF16), wd_ref[lo:lo + ff_chunk, :])
    x2 = _layer_norm(ALPHA * x + acc, lnw_ref[...], lnb_ref[...])
    e = _dot(p_ref[...].astype(BF16), wproj_ref[...]) * _sigmoid(_dot(x2.astype(BF16), wgate_ref[...]))
    o_ref[...] = x2 + _rms_norm(e, nw_ref[...])


def _dense_call(x, p_all, layer, wgu, wd, lnw, lnb, wproj, wgate, nw, *, tm):
    n, d = x.shape
    d_ff = wd.shape[1]
    ple = p_all.shape[-1]
    row = lambda i: (i, 0)
    kern = functools.partial(_dense_kernel, d_ff=d_ff, ff_chunk=MXU_TILE)
    return pl.pallas_call(
        kern,
        grid=(n // tm,),
        in_specs=[pl.BlockSpec((tm, d), row), pl.BlockSpec((None, tm, ple), lambda i: (layer, i, 0)),
                  _layer_block(wgu.shape, layer), _layer_block(wd.shape, layer),
                  _layer_block(lnw.shape, layer), _layer_block(lnb.shape, layer),
                  _layer_block(wproj.shape, layer), _layer_block(wgate.shape, layer),
                  _layer_block(nw.shape, layer)],
        out_specs=pl.BlockSpec((tm, d), row),
        out_shape=jax.ShapeDtypeStruct((n, d), F32),
        compiler_params=pltpu.CompilerParams(dimension_semantics=("parallel",),
                                             vmem_limit_bytes=VMEM_LIMIT_BYTES),
        name="dense_ffn_ple",
    )(x, p_all, wgu, wd, lnw, lnb, wproj, wgate, nw)


def _s5_kernel(*refs, bt, tc, nsub, has_state):
    if has_state:
        x_ref, h0r_ref, h0i_ref = refs[:3]
        refs = refs[3:]
    else:
        x_ref = refs[0]
        refs = refs[1:]
    (lamr_ref, lami_ref, bbr_ref, bbi_ref, cc_ref, dsk_ref, wglu_ref, lnw_ref, lnb_ref,
     o_ref, hro_ref, hio_ref,
     bur_ref, bui_ref, hb_ref, hsr_ref, hsi_ref, bdr_ref, bdi_ref, cdt_ref, slab_ref) = refs
    d = D_MODEL
    n_slab = d // LANES
    ti = pl.program_id(1)

    @pl.when(ti == 0)
    def _():
        if has_state:
            hsr_ref[...] = h0r_ref[...]
            hsi_ref[...] = h0i_ref[...]
        else:
            hsr_ref[...] = jnp.zeros(hsr_ref.shape, F32)
            hsi_ref[...] = jnp.zeros(hsi_ref.shape, F32)
        rep = MXU_TILE // S5_GROUP
        gi = lax.broadcasted_iota(jnp.int32, (MXU_TILE, S5_KT_COLS), 0) // S5_GROUP
        ci = lax.broadcasted_iota(jnp.int32, (MXU_TILE, S5_KT_COLS), 1) // S5_STATE
        in_mask = gi == ci
        out_mask = jnp.concatenate([in_mask, in_mask], axis=1)
        for kt in range(S5_KT):
            bdr_ref[kt] = jnp.where(in_mask, jnp.concatenate([bbr_ref[kt]] * rep, axis=0), 0.0).astype(BF16)
            bdi_ref[kt] = jnp.where(in_mask, jnp.concatenate([bbi_ref[kt]] * rep, axis=0), 0.0).astype(BF16)
            cdt_ref[kt] = jnp.where(out_mask, jnp.concatenate([cc_ref[kt]] * rep, axis=0), 0.0).astype(BF16)

    ts = tc // nsub

    def load_rows(k):
        return jnp.concatenate([slab_ref[k, s] for s in range(n_slab)], axis=1)

    def project_in(k):
        for s in range(n_slab):
            for b in range(bt):
                slab_ref[k, s, pl.ds(b, ts, stride=bt), :] = x_ref[b, k * ts:(k + 1) * ts, s * LANES:(s + 1) * LANES]
        ub = load_rows(k).astype(BF16)
        for kt in range(S5_KT):
            uk = ub[:, kt * MXU_TILE:(kt + 1) * MXU_TILE]
            bur_ref[k, :, kt * S5_KT_COLS:(kt + 1) * S5_KT_COLS] = _dot(uk, bdr_ref[kt])
            bui_ref[k, :, kt * S5_KT_COLS:(kt + 1) * S5_KT_COLS] = _dot(uk, bdi_ref[kt])

    def recurrence(k):
        cols = 4 * VREG_ELEMS // bt
        grp = max(1, 16 // bt)
        for s in range(S5_COLS // cols):
            cs = slice(s * cols, (s + 1) * cols)
            kt, off = divmod(s * cols, S5_KT_COLS)
            out_r = slice(kt * 2 * S5_KT_COLS + off, kt * 2 * S5_KT_COLS + off + cols)
            out_i = slice(out_r.start + S5_KT_COLS, out_r.stop + S5_KT_COLS)
            lr = jnp.broadcast_to(lamr_ref[:, cs], (bt, cols))
            li = jnp.broadcast_to(lami_ref[:, cs], (bt, cols))
            hr = hsr_ref[:, cs]
            hi = hsi_ref[:, cs]
            for t0 in range(0, ts, grp):
                hrs, his = [], []
                for t in range(t0, t0 + grp):
                    rs = slice(t * bt, (t + 1) * bt)
                    hr, hi = (lr * hr - li * hi + bur_ref[k, rs, cs],
                              lr * hi + li * hr + bui_ref[k, rs, cs])
                    hrs.append(hr)
                    his.append(hi)
                gs = slice(t0 * bt, (t0 + grp) * bt)
                hb_ref[k, gs, out_r] = jnp.concatenate(hrs, axis=0).astype(BF16)
                hb_ref[k, gs, out_i] = jnp.concatenate(his, axis=0).astype(BF16)
            hsr_ref[:, cs] = hr
            hsi_ref[:, cs] = hi

    def project_out(k):
        u = load_rows(k)
        ys = []
        for nt in range(S5_KT):
            hcat = hb_ref[k, :, nt * 2 * S5_KT_COLS:(nt + 1) * 2 * S5_KT_COLS]
            ys.append(lax.dot_general(hcat, cdt_ref[nt], NT_DIMS, preferred_element_type=F32))
        y = jnp.concatenate(ys, axis=1) + dsk_ref[...] * u
        y = y * (0.5 * (1.0 + jnp.tanh(math.sqrt(2.0 / math.pi) * (y + 0.044715 * (y * y * y)))))
        z = _dot(y.astype(BF16), wglu_ref[...])
        mix = z[:, :d] * _sigmoid(z[:, d:])
        xn = _layer_norm(ALPHA * u + mix, lnw_ref[...], lnb_ref[...])
        for s in range(n_slab):
            slab_ref[k, s] = xn[:, s * LANES:(s + 1) * LANES]
        for b in range(bt):
            for s in range(n_slab):
                o_ref[b, k * ts:(k + 1) * ts, s * LANES:(s + 1) * LANES] = slab_ref[k, s, pl.ds(b, ts, stride=bt), :]

    for k in range(nsub):
        project_in(k)
    for k in range(nsub):
        recurrence(k)
    hro_ref[...] = hsr_ref[...]
    hio_ref[...] = hsi_ref[...]
    for k in range(nsub):
        project_out(k)


def _s5_call(x, h0, layer, hprev, lamr, lami, bbr, bbi, cc, dsk, wglu, lnw, lnb, mix_layer, *, bt, tc, nsub):
    bsz, t_len, d = x.shape
    n_layers = lamr.shape[0]
    has_state = h0 is not None
    rows = bt * tc // nsub
    xspec = pl.BlockSpec((bt, tc, d), lambda b, t: (b, t, 0))
    sspec = pl.BlockSpec((None, bt, S5_COLS), lambda b, t: (layer, b, 0))
    any_spec = pl.BlockSpec(memory_space=pl.ANY)
    in_specs = [xspec] + ([sspec, sspec] if has_state else []) + [
        _layer_block(lamr.shape, layer), _layer_block(lami.shape, layer), _layer_block(bbr.shape, layer),
        _layer_block(bbi.shape, layer), _layer_block(cc.shape, layer), _layer_block(dsk.shape, layer),
        _layer_block(wglu.shape, layer), _layer_block(lnw.shape, mix_layer), _layer_block(lnb.shape, mix_layer)]
    args = [x] + (list(h0) if has_state else []) + [lamr, lami, bbr, bbi, cc, dsk, wglu, lnw, lnb]
    aliases = {}
    if hprev is not None:
        aliases = {len(args): 1, len(args) + 1: 2}
        in_specs += [any_spec, any_spec]
        args += list(hprev)
    kern = functools.partial(_s5_kernel_entry, bt=bt, tc=tc, nsub=nsub, has_state=has_state,
                             n_extra=len(aliases))
    st_shape = jax.ShapeDtypeStruct((n_layers, bsz, S5_COLS), F32)
    return pl.pallas_call(
        kern,
        grid=(bsz // bt, t_len // tc),
        in_specs=in_specs,
        out_specs=[xspec, sspec, sspec],
        out_shape=[jax.ShapeDtypeStruct(x.shape, F32), st_shape, st_shape],
        scratch_shapes=[pltpu.VMEM((nsub, rows, S5_COLS), F32), pltpu.VMEM((nsub, rows, S5_COLS), F32),
                        pltpu.VMEM((nsub, rows, 2 * S5_COLS), BF16),
                        pltpu.VMEM((bt, S5_COLS), F32), pltpu.VMEM((bt, S5_COLS), F32),
                        pltpu.VMEM((S5_KT, MXU_TILE, S5_KT_COLS), BF16),
                        pltpu.VMEM((S5_KT, MXU_TILE, S5_KT_COLS), BF16),
                        pltpu.VMEM((S5_KT, MXU_TILE, 2 * S5_KT_COLS), BF16),
                        pltpu.VMEM((nsub, d // LANES, rows, LANES), F32)],
        input_output_aliases=aliases,
        compiler_params=pltpu.CompilerParams(dimension_semantics=("parallel", "arbitrary"),
                                             vmem_limit_bytes=VMEM_LIMIT_BYTES),
        name="s5_mixer",
    )(*args)


def _s5_kernel_entry(*refs, has_state, n_extra, **kw):
    n_in = 1 + (2 if has_state else 0) + 9
    _s5_kernel(*(refs[:n_in] + refs[n_in + n_extra:]), has_state=has_state, **kw)


def _s5_params(a_re, a_im, b_re, b_im, c_re, c_im, log_step):
    n_l = a_re.shape[0]
    delta = jnp.exp(log_step)[..., None]
    er = jnp.exp(a_re * delta)
    lr = er * jnp.cos(a_im * delta)
    li = er * jnp.sin(a_im * delta)
    den = a_re * a_re + a_im * a_im
    qr = ((lr - 1.0) * a_re + li * a_im) / den
    qi = (li * a_re - (lr - 1.0) * a_im) / den
    bbr = qr[..., None] * b_re - qi[..., None] * b_im
    bbi = qr[..., None] * b_im + qi[..., None] * b_re

    def compact_in(bb):
        w = bb.reshape(n_l, S5_KT, S5_GROUPS_PER_TILE, S5_STATE, S5_GROUP).transpose(0, 1, 4, 2, 3)
        return w.reshape(n_l, S5_KT, S5_GROUP, S5_KT_COLS)

    def compact_out(cm):
        w = cm.reshape(n_l, S5_KT, S5_GROUPS_PER_TILE, S5_GROUP, S5_STATE).transpose(0, 1, 3, 2, 4)
        return w.reshape(n_l, S5_KT, S5_GROUP, S5_KT_COLS)

    cc = jnp.concatenate([compact_out(c_re), -compact_out(c_im)], axis=-1)
    return (lr.reshape(n_l, 1, S5_COLS), li.reshape(n_l, 1, S5_COLS), compact_in(bbr), compact_in(bbi), cc)


def _hgrn_kernel(*refs, nseq, seg, c, seq):
    if seq:
        (x_ref, win_ref, lb_ref, gn_ref, wout_ref, lnw_ref, lnb_ref,
         o_ref, so_ref, st_ref, qd_ref, kd_ref, v_ref, e_ref, oi_ref) = refs
        s0_ref = None
    else:
        (x_ref, s0_ref, win_ref, lb_ref, gn_ref, wout_ref, lnw_ref, lnb_ref,
         o_ref, so_ref, qd_ref, kd_ref, v_ref, e_ref, oi_ref) = refs
        st_ref = None
    d = D_MODEL
    hd = HG_HEAD_DIM
    nsub = seg // c
    lg = c.bit_length() - 1

    if seq:
        ti = pl.program_id(1)

        @pl.when(ti == 0)
        def _():
            st_ref[...] = jnp.zeros(st_ref.shape, F32)

    ri = lax.broadcasted_iota(jnp.int32, (seg, seg), 0)
    ci = lax.broadcasted_iota(jnp.int32, (seg, seg), 1)
    causal = jnp.logical_and((ri >> lg) == (ci >> lg), ci <= ri)
    tri = jnp.where(causal, 1.0, 0.0).astype(BF16)
    pair = seq
    cl = 2 * c if pair else c
    nloop = seg // cl
    cross = jnp.logical_and(((ri >> lg) & 1) == 1, (ci >> lg) == (ri >> lg) - 1)
    lb = lb_ref[...]
    ones_rows = jnp.ones((8, hd), BF16)

    def project(s):
        x = x_ref[s * seg:(s + 1) * seg, :]
        return x, _dot(x.astype(BF16), win_ref[...])

    def gates_and_local_attention(s, proj):
        rs = slice(s * seg, (s + 1) * seg)
        q = proj[:, :d]
        fz = proj[:, d:2 * d]
        v = proj[:, 2 * d:3 * d]
        f = lb + (1.0 - lb) * _sigmoid_rel(fz)
        logf = jnp.log(f)
        kk = 1.0 - f
        qs = q * _sigmoid(q) * HG_QK_SCALE

        l_hi = logf.astype(BF16)
        l_lo = (logf - l_hi.astype(F32)).astype(BF16)
        b = _dot(tri, l_hi) + _dot(tri, l_lo)
        b3 = b.reshape(nsub, c, d)
        bm = b3[:, c // 2 - 1:c // 2, :]
        bl = b3[:, c - 1:c, :]
        d1 = b3 - bm
        qh3 = qs.reshape(nsub, c, d) * jnp.exp(d1)
        kh3 = kk.reshape(nsub, c, d) * jnp.exp(-d1)
        qd3 = qh3 * jnp.exp(bm)
        kd3 = kh3 * jnp.exp(bl - bm)
        e3 = jnp.exp(bl)
        qh = qh3.reshape(seg, d).astype(BF16)
        kh = kh3.reshape(seg, d).astype(BF16)
        vb = v.astype(BF16)
        v_ref[rs, :] = v.astype(v_ref.dtype)
        if pair:
            qd4 = qd3.reshape(nloop, 2, c, d)
            kd4 = kd3.reshape(nloop, 2, c, d)
            e4 = e3.reshape(nloop, 2, 1, d)
            qd_loop = jnp.concatenate([qd4[:, 0:1], qd4[:, 1:2] * e4[:, 0:1]], axis=1)
            kd_loop = jnp.concatenate([kd4[:, 0:1] * e4[:, 1:2], kd4[:, 1:2]], axis=1)
            qd_ref[rs, :] = qd_loop.reshape(seg, d).astype(qd_ref.dtype)
            kd_ref[rs, :] = kd_loop.reshape(seg, d).astype(kd_ref.dtype)
            e_ref[s * nloop:(s + 1) * nloop] = e4[:, 0] * e4[:, 1]
            qdb = qd3.reshape(seg, d).astype(BF16)
            kdb = kd3.reshape(seg, d).astype(BF16)
        else:
            qd_ref[rs, :] = qd3.reshape(seg, d).astype(qd_ref.dtype)
            kd_ref[rs, :] = kd3.reshape(seg, d).astype(kd_ref.dtype)
            e_ref[s * nloop:(s + 1) * nloop] = e3

        for h in range(HG_HEADS):
            cs = slice(h * hd, (h + 1) * hd)
            att = lax.dot_general(qh[:, cs], kh[:, cs], NT_DIMS, preferred_element_type=F32)
            att = jnp.where(causal, att, 0.0)
            if pair:
                att1 = lax.dot_general(qdb[:, cs], kdb[:, cs], NT_DIMS, preferred_element_type=F32)
                att = att + jnp.where(cross, att1, 0.0)
            oi_ref[rs, cs] = _dot(att.astype(BF16), vb[:, cs])

    def state_recurrence(s):
        for j in range(nloop):
            r0 = s * seg + j * cl
            for h in range(HG_HEADS):
                cs = slice(h * hd, (h + 1) * hd)
                qd = qd_ref[pl.ds(r0, cl), cs].astype(BF16)
                kd = kd_ref[pl.ds(r0, cl), cs].astype(BF16)
                vv = v_ref[pl.ds(r0, cl), cs].astype(BF16)
                ej = e_ref[s * nloop + j, :, cs]
                if seq:
                    st = st_ref[h]
                    o_int = lax.dot_general(qd, st.astype(BF16), NT_DIMS, preferred_element_type=F32)
                    kv_t = lax.dot_general(vv, kd, TN_DIMS, preferred_element_type=F32)
                    st_ref[h] = st * ej + kv_t
                else:
                    st = s0_ref[j, h]
                    o_int = _dot(qd, st.astype(BF16))
                    e_col = lax.dot_general(_bf16_split_rows(ej), ones_rows, TN_DIMS,
                                            preferred_element_type=F32)
                    kv = lax.dot_general(kd, vv, TN_DIMS, preferred_element_type=F32)
                    so_ref[j, h] = st * e_col + kv
                oi_ref[pl.ds(r0, cl), cs] = oi_ref[pl.ds(r0, cl), cs] + o_int

    def output(s, x, proj):
        g = proj[:, 3 * d:]
        o = oi_ref[s * seg:(s + 1) * seg, :]
        outs = []
        for h in range(HG_HEADS):
            oh = o[:, h * hd:(h + 1) * hd]
            outs.append(oh * lax.rsqrt(jnp.mean(oh * oh, axis=-1, keepdims=True) + RMS_EPS))
        y = jnp.concatenate(outs, axis=1) * gn_ref[...] * (g * _sigmoid(g))
        mix = _dot(y.astype(BF16), wout_ref[...])
        o_ref[s * seg:(s + 1) * seg, :] = _layer_norm(ALPHA * x + mix, lnw_ref[...], lnb_ref[...])

    projected = [project(s) for s in range(nseq)]
    for s in range(nseq):
        gates_and_local_attention(s, projected[s][1])
    for s in range(nseq):
        state_recurrence(s)
    for s in range(nseq):
        output(s, *projected[s])

    if seq:
        @pl.when(ti == pl.num_programs(1) - 1)
        def _():
            for h in range(HG_HEADS):
                so_ref[h] = st_ref[h].T


def _hgrn_kernel_entry(*refs, n_in, n_extra, **kw):
    _hgrn_kernel(*(refs[:n_in] + refs[n_in + n_extra:]), **kw)


def _hgrn_scratch(rows, c, operand_dtype):
    d = D_MODEL
    return [pltpu.VMEM((rows, d), operand_dtype), pltpu.VMEM((rows, d), operand_dtype),
            pltpu.VMEM((rows, d), operand_dtype), pltpu.VMEM((rows // c, 1, d), F32),
            pltpu.VMEM((rows, d), F32)]


def _hgrn_call(x, s0, layer, sprev, win, lb, gn, wout, lnw, lnb, mix_layer, *, tile, nseq=1):
    bsz, t_len, d = x.shape
    hd = HG_HEAD_DIM
    n_layers = win.shape[0]
    seq = s0 is None
    weights = [win, lb, gn, wout]
    w_specs = [_layer_block(w.shape, layer) for w in weights] + [
        _layer_block(lnw.shape, mix_layer), _layer_block(lnb.shape, mix_layer)]
    any_spec = pl.BlockSpec(memory_space=pl.ANY)
    st_shape = jax.ShapeDtypeStruct((n_layers, bsz, HG_HEADS, hd, hd), F32)
    if seq:
        seg, c = tile, HG_CHUNK
        rows = nseq * seg
        grid = (bsz, t_len // rows)
        xspec = pl.BlockSpec((None, rows, d), lambda b, t: (b, t, 0))
        sspec = pl.BlockSpec((None, None, HG_HEADS, hd, hd), lambda b, t: (layer, b, 0, 0, 0))
        in_specs, args = [xspec] + w_specs, [x] + weights + [lnw, lnb]
        scratch = [pltpu.VMEM((HG_HEADS, hd, hd), F32)] + _hgrn_scratch(rows, c, BF16)
        sem = ("parallel", "arbitrary")
        x_in, x_shape = x, x.shape
    else:
        rows, c, nseq = tile * t_len, t_len, 1
        seg = rows
        grid = (bsz // tile,)
        xspec = pl.BlockSpec((rows, d), lambda i: (i, 0))
        sspec = pl.BlockSpec((None, tile, HG_HEADS, hd, hd), lambda i: (layer, i, 0, 0, 0))
        x_in, x_shape = x.reshape(bsz * t_len, d), (bsz * t_len, d)
        in_specs, args = [xspec, sspec] + w_specs, [x_in, s0] + weights + [lnw, lnb]
        scratch = _hgrn_scratch(rows, c, F32)
        sem = ("parallel",)
    n_in = len(args)
    aliases = {}
    if sprev is not None:
        aliases = {n_in: 1}
        in_specs, args = in_specs + [any_spec], args + [sprev]
    kern = functools.partial(_hgrn_kernel_entry, n_in=n_in, n_extra=len(aliases), nseq=nseq, seg=seg, c=c,
                             seq=seq)
    xo, so = pl.pallas_call(
        kern,
        grid=grid,
        in_specs=in_specs,
        out_specs=[xspec, sspec],
        out_shape=[jax.ShapeDtypeStruct(x_shape, F32), st_shape],
        scratch_shapes=scratch,
        input_output_aliases=aliases,
        compiler_params=pltpu.CompilerParams(dimension_semantics=sem, vmem_limit_bytes=VMEM_LIMIT_BYTES),
        name="hgrn_mixer_prompt" if seq else "hgrn_mixer_sample",
    )(*args)
    return xo.reshape(bsz, t_len, d), so


def _tile(n, target):
    t = min(n, target)
    while n % t:
        t //= 2
    return t


def kernel(x_prompt, x_sample, state_hgrn, state_s5_re, state_s5_im, p_prompt, p_sample, hg_w_in, hg_lower_bounds, hg_gnorm_w, hg_w_out, s5_a_re, s5_a_im, s5_b_re, s5_b_im, s5_c_re, s5_c_im, s5_d, s5_log_step, s5_w_glu, ln_mix_w, ln_mix_b, ffn_w_gate_up, ffn_w_down, ln_ffn_w, ln_ffn_b, ple_w_proj, ple_w_gate, ple_norm_w):
    bp, tp, d = x_prompt.shape
    bs, ts, _ = x_sample.shape
    depth = ln_mix_w.shape[0]
    ple = p_prompt.shape[-1]
    n_s5 = s5_a_re.shape[0]
    row3 = lambda a: a.reshape(a.shape[0], 1, -1)

    lb_soft = jax.nn.softmax(hg_lower_bounds.astype(F32), axis=0)
    lb_all = row3(jnp.cumsum(lb_soft, axis=0) - lb_soft[0])
    gn_all = row3(jnp.tile(hg_gnorm_w, (1, HG_HEADS)))
    hg_win, hg_wout = hg_w_in.astype(BF16), hg_w_out.astype(BF16)
    lamr, lami, bbr, bbi, cc = _s5_params(s5_a_re, s5_a_im, s5_b_re, s5_b_im, s5_c_re, s5_c_im, s5_log_step)
    dsk_all, wglu_all = row3(s5_d), s5_w_glu.astype(BF16)
    lnm_w, lnm_b = row3(ln_mix_w), row3(ln_mix_b)
    dense_w = (ffn_w_gate_up.astype(BF16), ffn_w_down.astype(BF16), row3(ln_ffn_w), row3(ln_ffn_b),
               ple_w_proj.astype(BF16), ple_w_gate.astype(BF16), row3(ple_norm_w))
    pp = p_prompt.reshape(depth, bp * tp, ple)
    ps = p_sample.reshape(depth, bs * ts, ple)
    h0_s = (state_s5_re.reshape(n_s5, bs, S5_COLS), state_s5_im.reshape(n_s5, bs, S5_COLS))

    xp, xs = x_prompt, x_sample
    hg_p = hg_s = s5_p = s5_s = None
    for i in range(depth):
        j = i // 2
        if i % 2 == 0:
            hg_args = (hg_win, lb_all, gn_all, hg_wout, lnm_w, lnm_b, i)
            xp, hg_p = _hgrn_call(xp, None, j, hg_p, *hg_args, tile=_tile(tp, 256), nseq=2)
            xs, hg_s = _hgrn_call(xs, state_hgrn, j, hg_s, *hg_args, tile=_tile(bs, 16))
        else:
            s5_args = (lamr, lami, bbr, bbi, cc, dsk_all, wglu_all, lnm_w, lnm_b, i)
            xp, *s5_p = _s5_call(xp, None, j, s5_p, *s5_args, bt=bp, tc=_tile(tp, 512 // bp), nsub=2)
            xs, *s5_s = _s5_call(xs, h0_s, j, s5_s, *s5_args, bt=_tile(bs, 256 // ts), tc=ts, nsub=1)
        xp = _dense_call(xp.reshape(bp * tp, d), pp, i, *dense_w, tm=_tile(bp * tp, 512)).reshape(bp, tp, d)
        xs = _dense_call(xs.reshape(bs * ts, d), ps, i, *dense_w, tm=_tile(bs * ts, 512)).reshape(bs, ts, d)
    s5_shape = lambda a, n: a.reshape(n_s5, n, S5_GROUPS, S5_STATE)
    return (xp, xs, hg_p, s5_shape(s5_p[0], bp), s5_shape(s5_p[1], bp),
            hg_s, s5_shape(s5_s[0], bs), s5_shape(s5_s[1], bs))
```

```python
import functools
import math

import jax
import jax.numpy as jnp
from jax import lax
from jax.experimental import pallas as pl
from jax.experimental.pallas import tpu as pltpu

F32 = jnp.float32
BF16 = jnp.bfloat16

D_MODEL = 1024
HG_HEAD_DIM = 128
HG_HEADS = D_MODEL // HG_HEAD_DIM
HG_QK_SCALE = HG_HEAD_DIM ** -0.5
HG_CHUNK = 16
S5_GROUP = 16
S5_GROUPS = D_MODEL // S5_GROUP
S5_STATE = 64
S5_COLS = S5_GROUPS * S5_STATE
DEPTH = 4
ALPHA = (2 * DEPTH) ** 0.25
LN_EPS = 1e-5
RMS_EPS = 1e-6

V7X_VMEM_BYTES = 64 * 1024 * 1024
VMEM_LIMIT_BYTES = V7X_VMEM_BYTES - 8 * 1024 * 1024
MXU_TILE = 256
LANES = 128
VREG_ELEMS = 8 * LANES

S5_GROUPS_PER_TILE = MXU_TILE // S5_GROUP
S5_KT = S5_GROUPS // S5_GROUPS_PER_TILE
S5_KT_COLS = S5_GROUPS_PER_TILE * S5_STATE

NT_DIMS = (((1,), (1,)), ((), ()))
TN_DIMS = (((0,), (0,)), ((), ()))


def _dot(a, b):
    return jnp.dot(a, b, preferred_element_type=F32)


def _sigmoid(x):
    return 0.5 * jnp.tanh(0.5 * x) + 0.5


def _sigmoid_rel(x):
    return 1.0 / (1.0 + jnp.exp(-x))


def _layer_norm(x, w, b):
    mu = jnp.mean(x, axis=-1, keepdims=True)
    xc = x - mu
    var = jnp.mean(xc * xc, axis=-1, keepdims=True)
    return xc * lax.rsqrt(var + LN_EPS) * w + b


def _rms_norm(x, w):
    return x * lax.rsqrt(jnp.mean(x * x, axis=-1, keepdims=True) + RMS_EPS) * w


def _bf16_split_rows(e):
    hi = e.astype(BF16).astype(F32)
    r1 = e - hi
    mid = r1.astype(BF16).astype(F32)
    lo = (r1 - mid).astype(BF16).astype(F32)
    row = lax.broadcasted_iota(jnp.int32, (8, e.shape[1]), 0)
    return jnp.where(row == 0, hi, jnp.where(row == 1, mid, jnp.where(row == 2, lo, 0.0))).astype(BF16)


def _resident(shape):
    nd = len(shape)
    return pl.BlockSpec(shape, lambda *_: (0,) * nd, pipeline_mode=pl.Buffered(1))


def _layer_block(shape, layer):
    nd = len(shape)
    return pl.BlockSpec((None,) + tuple(shape[1:]), lambda *_: (layer,) + (0,) * (nd - 1),
                        pipeline_mode=pl.Buffered(1))


def _dense_kernel(x_ref, p_ref, wgu_ref, wd_ref, lnw_ref, lnb_ref, wproj_ref, wgate_ref, nw_ref, o_ref,
                  *, d_ff, ff_chunk):
    x = x_ref[...]
    xb = x.astype(BF16)
    acc = jnp.zeros(x.shape, F32)
    for j in range(d_ff // ff_chunk):
        lo = j * ff_chunk
        g = _dot(xb, wgu_ref[:, lo:lo + ff_chunk])
        u = _dot(xb, wgu_ref[:, d_ff + lo:d_ff + lo + ff_chunk])
        h = (g * _sigmoid(g)) * u
        acc = acc + _dot(h.astype(BF16), wd_ref[lo:lo + ff_chunk, :])
    x2 = _layer_norm(ALPHA * x + acc, lnw_ref[...], lnb_ref[...])
    e = _dot(p_ref[...].astype(BF16), wproj_ref[...]) * _sigmoid(_dot(x2.astype(BF16), wgate_ref[...]))
    o_ref[...] = x2 + _rms_norm(e, nw_ref[...])


def _dense_call(x, p_all, layer, wgu, wd, lnw, lnb, wproj, wgate, nw, *, tm):
    n, d = x.shape
    d_ff = wd.shape[1]
    ple = p_all.shape[-1]
    row = lambda i: (i, 0)
    kern = functools.partial(_dense_kernel, d_ff=d_ff, ff_chunk=MXU_TILE)
    return pl.pallas_call(
        kern,
        grid=(n // tm,),
        in_specs=[pl.BlockSpec((tm, d), row), pl.BlockSpec((None, tm, ple), lambda i: (layer, i, 0)),
                  _layer_block(wgu.shape, layer), _layer_block(wd.shape, layer),
                  _layer_block(lnw.shape, layer), _layer_block(lnb.shape, layer),
                  _layer_block(wproj.shape, layer), _layer_block(wgate.shape, layer),
                  _layer_block(nw.shape, layer)],
        out_specs=pl.BlockSpec((tm, d), row),
        out_shape=jax.ShapeDtypeStruct((n, d), F32),
        compiler_params=pltpu.CompilerParams(dimension_semantics=("parallel",),
                                             vmem_limit_bytes=VMEM_LIMIT_BYTES),
        name="dense_ffn_ple",
    )(x, p_all, wgu, wd, lnw, lnb, wproj, wgate, nw)


def _s5_kernel(*refs, bt, tc, nsub, has_state):
    if has_state:
        x_ref, h0r_ref, h0i_ref = refs[:3]
        refs = refs[3:]
    else:
        x_ref = refs[0]
        refs = refs[1:]
    (lamr_ref, lami_ref, bbr_ref, bbi_ref, cc_ref, dsk_ref, wglu_ref, lnw_ref, lnb_ref,
     o_ref, hro_ref, hio_ref,
     bur_ref, bui_ref, hsr_ref, hsi_ref, bdr_ref, bdi_ref, cdt_ref, slab_ref) = refs
    d = D_MODEL
    n_slab = d // LANES
    ti = pl.program_id(1)

    @pl.when(ti == 0)
    def _():
        if has_state:
            hsr_ref[...] = h0r_ref[...]
            hsi_ref[...] = h0i_ref[...]
        else:
            hsr_ref[...] = jnp.zeros(hsr_ref.shape, F32)
            hsi_ref[...] = jnp.zeros(hsi_ref.shape, F32)
        rep = MXU_TILE // S5_GROUP
        gi = lax.broadcasted_iota(jnp.int32, (MXU_TILE, S5_KT_COLS), 0) // S5_GROUP
        ci = lax.broadcasted_iota(jnp.int32, (MXU_TILE, S5_KT_COLS), 1) // S5_STATE
        in_mask = gi == ci
        out_mask = jnp.concatenate([in_mask, in_mask], axis=1)
        for kt in range(S5_KT):
            bdr_ref[kt] = jnp.where(in_mask, jnp.concatenate([bbr_ref[kt]] * rep, axis=0), 0.0).astype(BF16)
            bdi_ref[kt] = jnp.where(in_mask, jnp.concatenate([bbi_ref[kt]] * rep, axis=0), 0.0).astype(BF16)
            cdt_ref[kt] = jnp.where(out_mask, jnp.concatenate([cc_ref[kt]] * rep, axis=0), 0.0).astype(BF16)

    ts = tc // nsub

    def load_rows(k):
        return jnp.concatenate([slab_ref[k, s] for s in range(n_slab)], axis=1)

    def project_in(k):
        for s in range(n_slab):
            for b in range(bt):
                slab_ref[k, s, pl.ds(b, ts, stride=bt), :] = x_ref[b, k * ts:(k + 1) * ts, s * LANES:(s + 1) * LANES]
        ub = load_rows(k).astype(BF16)
        for kt in range(S5_KT):
            uk = ub[:, kt * MXU_TILE:(kt + 1) * MXU_TILE]
            bur_ref[k, :, kt * S5_KT_COLS:(kt + 1) * S5_KT_COLS] = _dot(uk, bdr_ref[kt])
            bui_ref[k, :, kt * S5_KT_COLS:(kt + 1) * S5_KT_COLS] = _dot(uk, bdi_ref[kt])

    def recurrence(k):
        cols = 4 * VREG_ELEMS // bt
        for s in range(S5_COLS // cols):
            cs = slice(s * cols, (s + 1) * cols)
            lr = jnp.broadcast_to(lamr_ref[:, cs], (bt, cols))
            li = jnp.broadcast_to(lami_ref[:, cs], (bt, cols))
            hr = hsr_ref[:, cs]
            hi = hsi_ref[:, cs]
            for t in range(ts):
                rs = slice(t * bt, (t + 1) * bt)
                nr = lr * hr - li * hi + bur_ref[k, rs, cs]
                ni = lr * hi + li * hr + bui_ref[k, rs, cs]
                bur_ref[k, rs, cs] = nr
                bui_ref[k, rs, cs] = ni
                hr, hi = nr, ni
            hsr_ref[:, cs] = hr
            hsi_ref[:, cs] = hi

    def project_out(k):
        u = load_rows(k)
        ys = []
        for nt in range(S5_KT):
            hcat = jnp.concatenate([bur_ref[k, :, nt * S5_KT_COLS:(nt + 1) * S5_KT_COLS],
                                    bui_ref[k, :, nt * S5_KT_COLS:(nt + 1) * S5_KT_COLS]], axis=1).astype(BF16)
            ys.append(lax.dot_general(hcat, cdt_ref[nt], NT_DIMS, preferred_element_type=F32))
        y = jnp.concatenate(ys, axis=1) + dsk_ref[...] * u
        y = y * (0.5 * (1.0 + jnp.tanh(math.sqrt(2.0 / math.pi) * (y + 0.044715 * (y * y * y)))))
        z = _dot(y.astype(BF16), wglu_ref[...])
        mix = z[:, :d] * _sigmoid(z[:, d:])
        xn = _layer_norm(ALPHA * u + mix, lnw_ref[...], lnb_ref[...])
        for s in range(n_slab):
            slab_ref[k, s] = xn[:, s * LANES:(s + 1) * LANES]
        for b in range(bt):
            for s in range(n_slab):
                o_ref[b, k * ts:(k + 1) * ts, s * LANES:(s + 1) * LANES] = slab_ref[k, s, pl.ds(b, ts, stride=bt), :]

    for k in range(nsub):
        project_in(k)
    for k in range(nsub):
        recurrence(k)
    hro_ref[...] = hsr_ref[...]
    hio_ref[...] = hsi_ref[...]
    for k in range(nsub):
        project_out(k)


def _s5_call(x, h0, layer, hprev, lamr, lami, bbr, bbi, cc, dsk, wglu, lnw, lnb, mix_layer, *, bt, tc, nsub):
    bsz, t_len, d = x.shape
    n_layers = lamr.shape[0]
    has_state = h0 is not None
    rows = bt * tc // nsub
    xspec = pl.BlockSpec((bt, tc, d), lambda b, t: (b, t, 0))
    sspec = pl.BlockSpec((None, bt, S5_COLS), lambda b, t: (layer, b, 0))
    any_spec = pl.BlockSpec(memory_space=pl.ANY)
    in_specs = [xspec] + ([sspec, sspec] if has_state else []) + [
        _layer_block(lamr.shape, layer), _layer_block(lami.shape, layer), _layer_block(bbr.shape, layer),
        _layer_block(bbi.shape, layer), _layer_block(cc.shape, layer), _layer_block(dsk.shape, layer),
        _layer_block(wglu.shape, layer), _layer_block(lnw.shape, mix_layer), _layer_block(lnb.shape, mix_layer)]
    args = [x] + (list(h0) if has_state else []) + [lamr, lami, bbr, bbi, cc, dsk, wglu, lnw, lnb]
    aliases = {}
    if hprev is not None:
        aliases = {len(args): 1, len(args) + 1: 2}
        in_specs += [any_spec, any_spec]
        args += list(hprev)
    kern = functools.partial(_s5_kernel_entry, bt=bt, tc=tc, nsub=nsub, has_state=has_state,
                             n_extra=len(aliases))
    st_shape = jax.ShapeDtypeStruct((n_layers, bsz, S5_COLS), F32)
    return pl.pallas_call(
        kern,
        grid=(bsz // bt, t_len // tc),
        in_specs=in_specs,
        out_specs=[xspec, sspec, sspec],
        out_shape=[jax.ShapeDtypeStruct(x.shape, F32), st_shape, st_shape],
        scratch_shapes=[pltpu.VMEM((nsub, rows, S5_COLS), F32), pltpu.VMEM((nsub, rows, S5_COLS), F32),
                        pltpu.VMEM((bt, S5_COLS), F32), pltpu.VMEM((bt, S5_COLS), F32),
                        pltpu.VMEM((S5_KT, MXU_TILE, S5_KT_COLS), BF16),
                        pltpu.VMEM((S5_KT, MXU_TILE, S5_KT_COLS), BF16),
                        pltpu.VMEM((S5_KT, MXU_TILE, 2 * S5_KT_COLS), BF16),
                        pltpu.VMEM((nsub, d // LANES, rows, LANES), F32)],
        input_output_aliases=aliases,
        compiler_params=pltpu.CompilerParams(dimension_semantics=("parallel", "arbitrary"),
                                             vmem_limit_bytes=VMEM_LIMIT_BYTES),
        name="s5_mixer",
    )(*args)


def _s5_kernel_entry(*refs, has_state, n_extra, **kw):
    n_in = 1 + (2 if has_state else 0) + 9
    _s5_kernel(*(refs[:n_in] + refs[n_in + n_extra:]), has_state=has_state, **kw)


def _s5_params(a_re, a_im, b_re, b_im, c_re, c_im, log_step):
    n_l = a_re.shape[0]
    delta = jnp.exp(log_step)[..., None]
    er = jnp.exp(a_re * delta)
    lr = er * jnp.cos(a_im * delta)
    li = er * jnp.sin(a_im * delta)
    den = a_re * a_re + a_im * a_im
    qr = ((lr - 1.0) * a_re + li * a_im) / den
    qi = (li * a_re - (lr - 1.0) * a_im) / den
    bbr = qr[..., None] * b_re - qi[..., None] * b_im
    bbi = qr[..., None] * b_im + qi[..., None] * b_re

    def compact_in(bb):
        w = bb.reshape(n_l, S5_KT, S5_GROUPS_PER_TILE, S5_STATE, S5_GROUP).transpose(0, 1, 4, 2, 3)
        return w.reshape(n_l, S5_KT, S5_GROUP, S5_KT_COLS)

    def compact_out(cm):
        w = cm.reshape(n_l, S5_KT, S5_GROUPS_PER_TILE, S5_GROUP, S5_STATE).transpose(0, 1, 3, 2, 4)
        return w.reshape(n_l, S5_KT, S5_GROUP, S5_KT_COLS)

    cc = jnp.concatenate([compact_out(c_re), -compact_out(c_im)], axis=-1)
    return (lr.reshape(n_l, 1, S5_COLS), li.reshape(n_l, 1, S5_COLS), compact_in(bbr), compact_in(bbi), cc)


def _hgrn_kernel(*refs, nseq, seg, c, seq):
    if seq:
        (x_ref, win_ref, lb_ref, gn_ref, wout_ref, lnw_ref, lnb_ref,
         o_ref, so_ref, st_ref, qd_ref, kd_ref, v_ref, e_ref, oi_ref) = refs
        s0_ref = None
    else:
        (x_ref, s0_ref, win_ref, lb_ref, gn_ref, wout_ref, lnw_ref, lnb_ref,
         o_ref, so_ref, qd_ref, kd_ref, v_ref, e_ref, oi_ref) = refs
        st_ref = None
    d = D_MODEL
    hd = HG_HEAD_DIM
    nsub = seg // c
    lg = c.bit_length() - 1

    if seq:
        ti = pl.program_id(1)

        @pl.when(ti == 0)
        def _():
            st_ref[...] = jnp.zeros(st_ref.shape, F32)

    ri = lax.broadcasted_iota(jnp.int32, (seg, seg), 0)
    ci = lax.broadcasted_iota(jnp.int32, (seg, seg), 1)
    causal = jnp.logical_and((ri >> lg) == (ci >> lg), ci <= ri)
    tri = jnp.where(causal, 1.0, 0.0).astype(BF16)
    pair = seq
    cl = 2 * c if pair else c
    nloop = seg // cl
    cross = jnp.logical_and(((ri >> lg) & 1) == 1, (ci >> lg) == (ri >> lg) - 1)
    lb = lb_ref[...]
    ones_rows = jnp.ones((8, hd), BF16)

    def project(s):
        x = x_ref[s * seg:(s + 1) * seg, :]
        return x, _dot(x.astype(BF16), win_ref[...])

    def gates_and_local_attention(s, proj):
        rs = slice(s * seg, (s + 1) * seg)
        q = proj[:, :d]
        fz = proj[:, d:2 * d]
        v = proj[:, 2 * d:3 * d]
        f = lb + (1.0 - lb) * _sigmoid_rel(fz)
        logf = jnp.log(f)
        kk = 1.0 - f
        qs = q * _sigmoid(q) * HG_QK_SCALE

        l_hi = logf.astype(BF16)
        l_lo = (logf - l_hi.astype(F32)).astype(BF16)
        b = _dot(tri, l_hi) + _dot(tri, l_lo)
        b3 = b.reshape(nsub, c, d)
        bm = b3[:, c // 2 - 1:c // 2, :]
        bl = b3[:, c - 1:c, :]
        d1 = b3 - bm
        qh3 = qs.reshape(nsub, c, d) * jnp.exp(d1)
        kh3 = kk.reshape(nsub, c, d) * jnp.exp(-d1)
        qd3 = qh3 * jnp.exp(bm)
        kd3 = kh3 * jnp.exp(bl - bm)
        e3 = jnp.exp(bl)
        qh = qh3.reshape(seg, d).astype(BF16)
        kh = kh3.reshape(seg, d).astype(BF16)
        vb = v.astype(BF16)
        v_ref[rs, :] = v.astype(v_ref.dtype)
        if pair:
            qd4 = qd3.reshape(nloop, 2, c, d)
            kd4 = kd3.reshape(nloop, 2, c, d)
            e4 = e3.reshape(nloop, 2, 1, d)
            qd_loop = jnp.concatenate([qd4[:, 0:1], qd4[:, 1:2] * e4[:, 0:1]], axis=1)
            kd_loop = jnp.concatenate([kd4[:, 0:1] * e4[:, 1:2], kd4[:, 1:2]], axis=1)
            qd_ref[rs, :] = qd_loop.reshape(seg, d).astype(qd_ref.dtype)
            kd_ref[rs, :] = kd_loop.reshape(seg, d).astype(kd_ref.dtype)
            e_ref[s * nloop:(s + 1) * nloop] = e4[:, 0] * e4[:, 1]
            qdb = qd3.reshape(seg, d).astype(BF16)
            kdb = kd3.reshape(seg, d).astype(BF16)
        else:
            qd_ref[rs, :] = qd3.reshape(seg, d).astype(qd_ref.dtype)
            kd_ref[rs, :] = kd3.reshape(seg, d).astype(kd_ref.dtype)
            e_ref[s * nloop:(s + 1) * nloop] = e3

        for h in range(HG_HEADS):
            cs = slice(h * hd, (h + 1) * hd)
            att = lax.dot_general(qh[:, cs], kh[:, cs], NT_DIMS, preferred_element_type=F32)
            att = jnp.where(causal, att, 0.0)
            if pair:
                att1 = lax.dot_general(qdb[:, cs], kdb[:, cs], NT_DIMS, preferred_element_type=F32)
                att = att + jnp.where(cross, att1, 0.0)
            oi_ref[rs, cs] = _dot(att.astype(BF16), vb[:, cs])

    def state_recurrence(s):
        for j in range(nloop):
            r0 = s * seg + j * cl
            for h in range(HG_HEADS):
                cs = slice(h * hd, (h + 1) * hd)
                qd = qd_ref[pl.ds(r0, cl), cs].astype(BF16)
                kd = kd_ref[pl.ds(r0, cl), cs].astype(BF16)
                vv = v_ref[pl.ds(r0, cl), cs].astype(BF16)
                ej = e_ref[s * nloop + j, :, cs]
                if seq:
                    st = st_ref[h]
                    o_int = lax.dot_general(qd, st.astype(BF16), NT_DIMS, preferred_element_type=F32)
                    kv_t = lax.dot_general(vv, kd, TN_DIMS, preferred_element_type=F32)
                    st_ref[h] = st * ej + kv_t
                else:
                    st = s0_ref[j, h]
                    o_int = _dot(qd, st.astype(BF16))
                    e_col = lax.dot_general(_bf16_split_rows(ej), ones_rows, TN_DIMS,
                                            preferred_element_type=F32)
                    kv = lax.dot_general(kd, vv, TN_DIMS, preferred_element_type=F32)
                    so_ref[j, h] = st * e_col + kv
                oi_ref[pl.ds(r0, cl), cs] = oi_ref[pl.ds(r0, cl), cs] + o_int

    def output(s, x, proj):
        g = proj[:, 3 * d:]
        o = oi_ref[s * seg:(s + 1) * seg, :]
        outs = []
        for h in range(HG_HEADS):
            oh = o[:, h * hd:(h + 1) * hd]
            outs.append(oh * lax.rsqrt(jnp.mean(oh * oh, axis=-1, keepdims=True) + RMS_EPS))
        y = jnp.concatenate(outs, axis=1) * gn_ref[...] * (g * _sigmoid(g))
        mix = _dot(y.astype(BF16), wout_ref[...])
        o_ref[s * seg:(s + 1) * seg, :] = _layer_norm(ALPHA * x + mix, lnw_ref[...], lnb_ref[...])

    projected = [project(s) for s in range(nseq)]
    for s in range(nseq):
        gates_and_local_attention(s, projected[s][1])
    for s in range(nseq):
        state_recurrence(s)
    for s in range(nseq):
        output(s, *projected[s])

    if seq:
        @pl.when(ti == pl.num_programs(1) - 1)
        def _():
            for h in range(HG_HEADS):
                so_ref[h] = st_ref[h].T


def _hgrn_kernel_entry(*refs, n_in, n_extra, **kw):
    _hgrn_kernel(*(refs[:n_in] + refs[n_in + n_extra:]), **kw)


def _hgrn_scratch(rows, c, operand_dtype):
    d = D_MODEL
    return [pltpu.VMEM((rows, d), operand_dtype), pltpu.VMEM((rows, d), operand_dtype),
            pltpu.VMEM((rows, d), operand_dtype), pltpu.VMEM((rows // c, 1, d), F32),
            pltpu.VMEM((rows, d), F32)]


def _hgrn_call(x, s0, layer, sprev, win, lb, gn, wout, lnw, lnb, mix_layer, *, tile, nseq=1):
    bsz, t_len, d = x.shape
    hd = HG_HEAD_DIM
    n_layers = win.shape[0]
    seq = s0 is None
    weights = [win, lb, gn, wout]
    w_specs = [_layer_block(w.shape, layer) for w in weights] + [
        _layer_block(lnw.shape, mix_layer), _layer_block(lnb.shape, mix_layer)]
    any_spec = pl.BlockSpec(memory_space=pl.ANY)
    st_shape = jax.ShapeDtypeStruct((n_layers, bsz, HG_HEADS, hd, hd), F32)
    if seq:
        seg, c = tile, HG_CHUNK
        rows = nseq * seg
        grid = (bsz, t_len // rows)
        xspec = pl.BlockSpec((None, rows, d), lambda b, t: (b, t, 0))
        sspec = pl.BlockSpec((None, None, HG_HEADS, hd, hd), lambda b, t: (layer, b, 0, 0, 0))
        in_specs, args = [xspec] + w_specs, [x] + weights + [lnw, lnb]
        scratch = [pltpu.VMEM((HG_HEADS, hd, hd), F32)] + _hgrn_scratch(rows, c, BF16)
        sem = ("parallel", "arbitrary")
        x_in, x_shape = x, x.shape
    else:
        rows, c, nseq = tile * t_len, t_len, 1
        seg = rows
        grid = (bsz // tile,)
        xspec = pl.BlockSpec((rows, d), lambda i: (i, 0))
        sspec = pl.BlockSpec((None, tile, HG_HEADS, hd, hd), lambda i: (layer, i, 0, 0, 0))
        x_in, x_shape = x.reshape(bsz * t_len, d), (bsz * t_len, d)
        in_specs, args = [xspec, sspec] + w_specs, [x_in, s0] + weights + [lnw, lnb]
        scratch = _hgrn_scratch(rows, c, F32)
        sem = ("parallel",)
    n_in = len(args)
    aliases = {}
    if sprev is not None:
        aliases = {n_in: 1}
        in_specs, args = in_specs + [any_spec], args + [sprev]
    kern = functools.partial(_hgrn_kernel_entry, n_in=n_in, n_extra=len(aliases), nseq=nseq, seg=seg, c=c,
                             seq=seq)
    xo, so = pl.pallas_call(
        kern,
        grid=grid,
        in_specs=in_specs,
        out_specs=[xspec, sspec],
        out_shape=[jax.ShapeDtypeStruct(x_shape, F32), st_shape],
        scratch_shapes=scratch,
        input_output_aliases=aliases,
        compiler_params=pltpu.CompilerParams(dimension_semantics=sem, vmem_limit_bytes=VMEM_LIMIT_BYTES),
        name="hgrn_mixer_prompt" if seq else "hgrn_mixer_sample",
    )(*args)
    return xo.reshape(bsz, t_len, d), so


def _tile(n, target):
    t = min(n, target)
    while n % t:
        t //= 2
    return t


def kernel(x_prompt, x_sample, state_hgrn, state_s5_re, state_s5_im, p_prompt, p_sample, hg_w_in, hg_lower_bounds, hg_gnorm_w, hg_w_out, s5_a_re, s5_a_im, s5_b_re, s5_b_im, s5_c_re, s5_c_im, s5_d, s5_log_step, s5_w_glu, ln_mix_w, ln_mix_b, ffn_w_gate_up, ffn_w_down, ln_ffn_w, ln_ffn_b, ple_w_proj, ple_w_gate, ple_norm_w):
    bp, tp, d = x_prompt.shape
    bs, ts, _ = x_sample.shape
    depth = ln_mix_w.shape[0]
    ple = p_prompt.shape[-1]
    n_s5 = s5_a_re.shape[0]
    row3 = lambda a: a.reshape(a.shape[0], 1, -1)

    lb_soft = jax.nn.softmax(hg_lower_bounds.astype(F32), axis=0)
    lb_all = row3(jnp.cumsum(lb_soft, axis=0) - lb_soft[0])
    gn_all = row3(jnp.tile(hg_gnorm_w, (1, HG_HEADS)))
    hg_win, hg_wout = hg_w_in.astype(BF16), hg_w_out.astype(BF16)
    lamr, lami, bbr, bbi, cc = _s5_params(s5_a_re, s5_a_im, s5_b_re, s5_b_im, s5_c_re, s5_c_im, s5_log_step)
    dsk_all, wglu_all = row3(s5_d), s5_w_glu.astype(BF16)
    lnm_w, lnm_b = row3(ln_mix_w), row3(ln_mix_b)
    dense_w = (ffn_w_gate_up.astype(BF16), ffn_w_down.astype(BF16), row3(ln_ffn_w), row3(ln_ffn_b),
               ple_w_proj.astype(BF16), ple_w_gate.astype(BF16), row3(ple_norm_w))
    pp = p_prompt.reshape(depth, bp * tp, ple)
    ps = p_sample.reshape(depth, bs * ts, ple)
    h0_s = (state_s5_re.reshape(n_s5, bs, S5_COLS), state_s5_im.reshape(n_s5, bs, S5_COLS))

    xp, xs = x_prompt, x_sample
    hg_p = hg_s = s5_p = s5_s = None
    for i in range(depth):
        j = i // 2
        if i % 2 == 0:
            hg_args = (hg_win, lb_all, gn_all, hg_wout, lnm_w, lnm_b, i)
            xp, hg_p = _hgrn_call(xp, None, j, hg_p, *hg_args, tile=_tile(tp, 256), nseq=2)
            xs, hg_s = _hgrn_call(xs, state_hgrn, j, hg_s, *hg_args, tile=_tile(bs, 16))
        else:
            s5_args = (lamr, lami, bbr, bbi, cc, dsk_all, wglu_all, lnm_w, lnm_b, i)
            xp, *s5_p = _s5_call(xp, None, j, s5_p, *s5_args, bt=bp, tc=_tile(tp, 512 // bp), nsub=2)
            xs, *s5_s = _s5_call(xs, h0_s, j, s5_s, *s5_args, bt=_tile(bs, 256 // ts), tc=ts, nsub=1)
        xp = _dense_call(xp.reshape(bp * tp, d), pp, i, *dense_w, tm=_tile(bp * tp, 1024)).reshape(bp, tp, d)
        xs = _dense_call(xs.reshape(bs * ts, d), ps, i, *dense_w, tm=_tile(bs * ts, 512)).reshape(bs, ts, d)
    s5_shape = lambda a, n: a.reshape(n_s5, n, S5_GROUPS, S5_STATE)
    return (xp, xs, hg_p, s5_shape(s5_p[0], bp), s5_shape(s5_p[1], bp),
            hg_s, s5_shape(s5_s[0], bs), s5_shape(s5_s[1], bs))
```

```python
import functools
import math

import jax
import jax.numpy as jnp
from jax import lax
from jax.experimental import pallas as pl
from jax.experimental.pallas import tpu as pltpu

F32 = jnp.float32
BF16 = jnp.bfloat16

D_MODEL = 1024
HG_HEAD_DIM = 128
HG_HEADS = D_MODEL // HG_HEAD_DIM
HG_QK_SCALE = HG_HEAD_DIM ** -0.5
HG_CHUNK = 16
S5_GROUP = 16
S5_GROUPS = D_MODEL // S5_GROUP
S5_STATE = 64
S5_COLS = S5_GROUPS * S5_STATE
DEPTH = 4
ALPHA = (2 * DEPTH) ** 0.25
LN_EPS = 1e-5
RMS_EPS = 1e-6

V7X_VMEM_BYTES = 64 * 1024 * 1024
VMEM_LIMIT_BYTES = V7X_VMEM_BYTES - 8 * 1024 * 1024
VMEM_LIMIT_LARGE_BYTES = V7X_VMEM_BYTES - 2 * 1024 * 1024
MXU_TILE = 256
LANES = 128
VREG_ELEMS = 8 * LANES

S5_GROUPS_PER_TILE = MXU_TILE // S5_GROUP
S5_KT = S5_GROUPS // S5_GROUPS_PER_TILE
S5_KT_COLS = S5_GROUPS_PER_TILE * S5_STATE

NT_DIMS = (((1,), (1,)), ((), ()))
TN_DIMS = (((0,), (0,)), ((), ()))


def _dot(a, b):
    return jnp.dot(a, b, preferred_element_type=F32)


def _sigmoid(x):
    return 0.5 * jnp.tanh(0.5 * x) + 0.5


def _sigmoid_rel(x):
    return 1.0 / (1.0 + jnp.exp(-x))


def _layer_norm(x, w, b):
    mu = jnp.mean(x, axis=-1, keepdims=True)
    xc = x - mu
    var = jnp.mean(xc * xc, axis=-1, keepdims=True)
    return xc * lax.rsqrt(var + LN_EPS) * w + b


def _rms_norm(x, w):
    return x * lax.rsqrt(jnp.mean(x * x, axis=-1, keepdims=True) + RMS_EPS) * w


def _bf16_split_rows(e):
    hi = e.astype(BF16).astype(F32)
    r1 = e - hi
    mid = r1.astype(BF16).astype(F32)
    lo = (r1 - mid).astype(BF16).astype(F32)
    row = lax.broadcasted_iota(jnp.int32, (8, e.shape[1]), 0)
    return jnp.where(row == 0, hi, jnp.where(row == 1, mid, jnp.where(row == 2, lo, 0.0))).astype(BF16)


def _resident(shape):
    nd = len(shape)
    return pl.BlockSpec(shape, lambda *_: (0,) * nd, pipeline_mode=pl.Buffered(1))


def _layer_block(shape, layer):
    nd = len(shape)
    return pl.BlockSpec((None,) + tuple(shape[1:]), lambda *_: (layer,) + (0,) * (nd - 1),
                        pipeline_mode=pl.Buffered(1))


def _dense_kernel(x_ref, p_ref, wgu_ref, wd_ref, lnw_ref, lnb_ref, wproj_ref, wgate_ref, nw_ref, o_ref,
                  *, d_ff, ff_chunk):
    x = x_ref[...]
    xb = x.astype(BF16)
    acc = jnp.zeros(x.shape, F32)
    for j in range(d_ff // ff_chunk):
        lo = j * ff_chunk
        g = _dot(xb, wgu_ref[:, lo:lo + ff_chunk])
        u = _dot(xb, wgu_ref[:, d_ff + lo:d_ff + lo + ff_chunk])
        h = (g * _sigmoid(g)) * u
        acc = acc + _dot(h.astype(BF16), wd_ref[lo:lo + ff_chunk, :])
    x2 = _layer_norm(ALPHA * x + acc, lnw_ref[...], lnb_ref[...])
    e = _dot(p_ref[...].astype(BF16), wproj_ref[...]) * _sigmoid(_dot(x2.astype(BF16), wgate_ref[...]))
    o_ref[...] = x2 + _rms_norm(e, nw_ref[...])


def _dense_call(x, p_all, layer, wgu, wd, lnw, lnb, wproj, wgate, nw, *, tm):
    n, d = x.shape
    d_ff = wd.shape[1]
    ple = p_all.shape[-1]
    row = lambda i: (i, 0)
    kern = functools.partial(_dense_kernel, d_ff=d_ff, ff_chunk=MXU_TILE)
    return pl.pallas_call(
        kern,
        grid=(n // tm,),
        in_specs=[pl.BlockSpec((tm, d), row), pl.BlockSpec((None, tm, ple), lambda i: (layer, i, 0)),
                  _layer_block(wgu.shape, layer), _layer_block(wd.shape, layer),
                  _layer_block(lnw.shape, layer), _layer_block(lnb.shape, layer),
                  _layer_block(wproj.shape, layer), _layer_block(wgate.shape, layer),
                  _layer_block(nw.shape, layer)],
        out_specs=pl.BlockSpec((tm, d), row),
        out_shape=jax.ShapeDtypeStruct((n, d), F32),
        compiler_params=pltpu.CompilerParams(dimension_semantics=("parallel",),
                                             vmem_limit_bytes=VMEM_LIMIT_BYTES),
        name="dense_ffn_ple",
    )(x, p_all, wgu, wd, lnw, lnb, wproj, wgate, nw)


def _s5_kernel(*refs, bt, tc, nsub, has_state):
    if has_state:
        x_ref, h0r_ref, h0i_ref = refs[:3]
        refs = refs[3:]
    else:
        x_ref = refs[0]
        refs = refs[1:]
    (lamr_ref, lami_ref, bbr_ref, bbi_ref, cc_ref, dsk_ref, wglu_ref, lnw_ref, lnb_ref,
     o_ref, hro_ref, hio_ref,
     bur_ref, bui_ref, hsr_ref, hsi_ref, bdr_ref, bdi_ref, cdt_ref, slab_ref) = refs
    d = D_MODEL
    n_slab = d // LANES
    ti = pl.program_id(1)

    @pl.when(ti == 0)
    def _():
        if has_state:
            hsr_ref[...] = h0r_ref[...]
            hsi_ref[...] = h0i_ref[...]
        else:
            hsr_ref[...] = jnp.zeros(hsr_ref.shape, F32)
            hsi_ref[...] = jnp.zeros(hsi_ref.shape, F32)
        rep = MXU_TILE // S5_GROUP
        gi = lax.broadcasted_iota(jnp.int32, (MXU_TILE, S5_KT_COLS), 0) // S5_GROUP
        ci = lax.broadcasted_iota(jnp.int32, (MXU_TILE, S5_KT_COLS), 1) // S5_STATE
        in_mask = gi == ci
        out_mask = jnp.concatenate([in_mask, in_mask], axis=1)
        for kt in range(S5_KT):
            bdr_ref[kt] = jnp.where(in_mask, jnp.concatenate([bbr_ref[kt]] * rep, axis=0), 0.0).astype(BF16)
            bdi_ref[kt] = jnp.where(in_mask, jnp.concatenate([bbi_ref[kt]] * rep, axis=0), 0.0).astype(BF16)
            cdt_ref[kt] = jnp.where(out_mask, jnp.concatenate([cc_ref[kt]] * rep, axis=0), 0.0).astype(BF16)

    ts = tc // nsub

    def load_rows(k):
        return jnp.concatenate([slab_ref[k, s] for s in range(n_slab)], axis=1)

    def project_in(k):
        for s in range(n_slab):
            for b in range(bt):
                slab_ref[k, s, pl.ds(b, ts, stride=bt), :] = x_ref[b, k * ts:(k + 1) * ts, s * LANES:(s + 1) * LANES]
        ub = load_rows(k).astype(BF16)
        for kt in range(S5_KT):
            uk = ub[:, kt * MXU_TILE:(kt + 1) * MXU_TILE]
            bur_ref[k, :, kt * S5_KT_COLS:(kt + 1) * S5_KT_COLS] = _dot(uk, bdr_ref[kt])
            bui_ref[k, :, kt * S5_KT_COLS:(kt + 1) * S5_KT_COLS] = _dot(uk, bdi_ref[kt])

    def recurrence(k):
        cols = 4 * VREG_ELEMS // bt
        for s in range(S5_COLS // cols):
            cs = slice(s * cols, (s + 1) * cols)
            lr = jnp.broadcast_to(lamr_ref[:, cs], (bt, cols))
            li = jnp.broadcast_to(lami_ref[:, cs], (bt, cols))
            hr = hsr_ref[:, cs]
            hi = hsi_ref[:, cs]
            for t in range(ts):
                rs = slice(t * bt, (t + 1) * bt)
                nr = lr * hr - li * hi + bur_ref[k, rs, cs]
                ni = lr * hi + li * hr + bui_ref[k, rs, cs]
                bur_ref[k, rs, cs] = nr
                bui_ref[k, rs, cs] = ni
                hr, hi = nr, ni
            hsr_ref[:, cs] = hr
            hsi_ref[:, cs] = hi

    def project_out(k):
        u = load_rows(k)
        ys = []
        for nt in range(S5_KT):
            hcat = jnp.concatenate([bur_ref[k, :, nt * S5_KT_COLS:(nt + 1) * S5_KT_COLS],
                                    bui_ref[k, :, nt * S5_KT_COLS:(nt + 1) * S5_KT_COLS]], axis=1).astype(BF16)
            ys.append(lax.dot_general(hcat, cdt_ref[nt], NT_DIMS, preferred_element_type=F32))
        y = jnp.concatenate(ys, axis=1) + dsk_ref[...] * u
        y = y * (0.5 * (1.0 + jnp.tanh(math.sqrt(2.0 / math.pi) * (y + 0.044715 * (y * y * y)))))
        z = _dot(y.astype(BF16), wglu_ref[...])
        mix = z[:, :d] * _sigmoid(z[:, d:])
        xn = _layer_norm(ALPHA * u + mix, lnw_ref[...], lnb_ref[...])
        for s in range(n_slab):
            slab_ref[k, s] = xn[:, s * LANES:(s + 1) * LANES]
        for b in range(bt):
            for s in range(n_slab):
                o_ref[b, k * ts:(k + 1) * ts, s * LANES:(s + 1) * LANES] = slab_ref[k, s, pl.ds(b, ts, stride=bt), :]

    for k in range(nsub):
        project_in(k)
    for k in range(nsub):
        recurrence(k)
    hro_ref[...] = hsr_ref[...]
    hio_ref[...] = hsi_ref[...]
    for k in range(nsub):
        project_out(k)


def _s5_call(x, h0, layer, hprev, lamr, lami, bbr, bbi, cc, dsk, wglu, lnw, lnb, mix_layer, *, bt, tc, nsub):
    bsz, t_len, d = x.shape
    n_layers = lamr.shape[0]
    has_state = h0 is not None
    rows = bt * tc // nsub
    xspec = pl.BlockSpec((bt, tc, d), lambda b, t: (b, t, 0))
    sspec = pl.BlockSpec((None, bt, S5_COLS), lambda b, t: (layer, b, 0))
    any_spec = pl.BlockSpec(memory_space=pl.ANY)
    in_specs = [xspec] + ([sspec, sspec] if has_state else []) + [
        _layer_block(lamr.shape, layer), _layer_block(lami.shape, layer), _layer_block(bbr.shape, layer),
        _layer_block(bbi.shape, layer), _layer_block(cc.shape, layer), _layer_block(dsk.shape, layer),
        _layer_block(wglu.shape, layer), _layer_block(lnw.shape, mix_layer), _layer_block(lnb.shape, mix_layer)]
    args = [x] + (list(h0) if has_state else []) + [lamr, lami, bbr, bbi, cc, dsk, wglu, lnw, lnb]
    aliases = {}
    if hprev is not None:
        aliases = {len(args): 1, len(args) + 1: 2}
        in_specs += [any_spec, any_spec]
        args += list(hprev)
    kern = functools.partial(_s5_kernel_entry, bt=bt, tc=tc, nsub=nsub, has_state=has_state,
                             n_extra=len(aliases))
    st_shape = jax.ShapeDtypeStruct((n_layers, bsz, S5_COLS), F32)
    return pl.pallas_call(
        kern,
        grid=(bsz // bt, t_len // tc),
        in_specs=in_specs,
        out_specs=[xspec, sspec, sspec],
        out_shape=[jax.ShapeDtypeStruct(x.shape, F32), st_shape, st_shape],
        scratch_shapes=[pltpu.VMEM((nsub, rows, S5_COLS), F32), pltpu.VMEM((nsub, rows, S5_COLS), F32),
                        pltpu.VMEM((bt, S5_COLS), F32), pltpu.VMEM((bt, S5_COLS), F32),
                        pltpu.VMEM((S5_KT, MXU_TILE, S5_KT_COLS), BF16),
                        pltpu.VMEM((S5_KT, MXU_TILE, S5_KT_COLS), BF16),
                        pltpu.VMEM((S5_KT, MXU_TILE, 2 * S5_KT_COLS), BF16),
                        pltpu.VMEM((nsub, d // LANES, rows, LANES), F32)],
        input_output_aliases=aliases,
        compiler_params=pltpu.CompilerParams(dimension_semantics=("parallel", "arbitrary"),
                                             vmem_limit_bytes=VMEM_LIMIT_BYTES),
        name="s5_mixer",
    )(*args)


def _s5_kernel_entry(*refs, has_state, n_extra, **kw):
    n_in = 1 + (2 if has_state else 0) + 9
    _s5_kernel(*(refs[:n_in] + refs[n_in + n_extra:]), has_state=has_state, **kw)


def _s5_params(a_re, a_im, b_re, b_im, c_re, c_im, log_step):
    n_l = a_re.shape[0]
    delta = jnp.exp(log_step)[..., None]
    er = jnp.exp(a_re * delta)
    lr = er * jnp.cos(a_im * delta)
    li = er * jnp.sin(a_im * delta)
    den = a_re * a_re + a_im * a_im
    qr = ((lr - 1.0) * a_re + li * a_im) / den
    qi = (li * a_re - (lr - 1.0) * a_im) / den
    bbr = qr[..., None] * b_re - qi[..., None] * b_im
    bbi = qr[..., None] * b_im + qi[..., None] * b_re

    def compact_in(bb):
        w = bb.reshape(n_l, S5_KT, S5_GROUPS_PER_TILE, S5_STATE, S5_GROUP).transpose(0, 1, 4, 2, 3)
        return w.reshape(n_l, S5_KT, S5_GROUP, S5_KT_COLS)

    def compact_out(cm):
        w = cm.reshape(n_l, S5_KT, S5_GROUPS_PER_TILE, S5_GROUP, S5_STATE).transpose(0, 1, 3, 2, 4)
        return w.reshape(n_l, S5_KT, S5_GROUP, S5_KT_COLS)

    cc = jnp.concatenate([compact_out(c_re), -compact_out(c_im)], axis=-1)
    return (lr.reshape(n_l, 1, S5_COLS), li.reshape(n_l, 1, S5_COLS), compact_in(bbr), compact_in(bbi), cc)


def _hgrn_kernel(*refs, nseq, seg, c, seq):
    if seq:
        (x_ref, win_ref, lb_ref, gn_ref, wout_ref, lnw_ref, lnb_ref,
         o_ref, so_ref, st_ref, qd_ref, kd_ref, v_ref, e_ref, oi_ref) = refs
        s0_ref = None
    else:
        (x_ref, s0_ref, win_ref, lb_ref, gn_ref, wout_ref, lnw_ref, lnb_ref,
         o_ref, so_ref, qd_ref, kd_ref, v_ref, e_ref, oi_ref) = refs
        st_ref = None
    d = D_MODEL
    hd = HG_HEAD_DIM
    nsub = seg // c
    lg = c.bit_length() - 1

    if seq:
        ti = pl.program_id(1)

        @pl.when(ti == 0)
        def _():
            st_ref[...] = jnp.zeros(st_ref.shape, F32)

    ri = lax.broadcasted_iota(jnp.int32, (seg, seg), 0)
    ci = lax.broadcasted_iota(jnp.int32, (seg, seg), 1)
    causal = jnp.logical_and((ri >> lg) == (ci >> lg), ci <= ri)
    tri = jnp.where(causal, 1.0, 0.0).astype(BF16)
    pair = seq
    cl = 2 * c if pair else c
    nloop = seg // cl
    cross = jnp.logical_and(((ri >> lg) & 1) == 1, (ci >> lg) == (ri >> lg) - 1)
    lb = lb_ref[...]
    ones_rows = jnp.ones((8, hd), BF16)

    def project(s):
        x = x_ref[s * seg:(s + 1) * seg, :]
        return x, _dot(x.astype(BF16), win_ref[...])

    def gates_and_local_attention(s, proj):
        rs = slice(s * seg, (s + 1) * seg)
        q = proj[:, :d]
        fz = proj[:, d:2 * d]
        v = proj[:, 2 * d:3 * d]
        f = lb + (1.0 - lb) * _sigmoid_rel(fz)
        logf = jnp.log(f)
        kk = 1.0 - f
        qs = q * _sigmoid(q) * HG_QK_SCALE

        l_hi = logf.astype(BF16)
        l_lo = (logf - l_hi.astype(F32)).astype(BF16)
        b = _dot(tri, l_hi) + _dot(tri, l_lo)
        b3 = b.reshape(nsub, c, d)
        bm = b3[:, c // 2 - 1:c // 2, :]
        bl = b3[:, c - 1:c, :]
        d1 = b3 - bm
        qh3 = qs.reshape(nsub, c, d) * jnp.exp(d1)
        kh3 = kk.reshape(nsub, c, d) * jnp.exp(-d1)
        qd3 = qh3 * jnp.exp(bm)
        kd3 = kh3 * jnp.exp(bl - bm)
        e3 = jnp.exp(bl)
        qh = qh3.reshape(seg, d).astype(BF16)
        kh = kh3.reshape(seg, d).astype(BF16)
        vb = v.astype(BF16)
        v_ref[rs, :] = v.astype(v_ref.dtype)
        if pair:
            qd4 = qd3.reshape(nloop, 2, c, d)
            kd4 = kd3.reshape(nloop, 2, c, d)
            e4 = e3.reshape(nloop, 2, 1, d)
            qd_loop = jnp.concatenate([qd4[:, 0:1], qd4[:, 1:2] * e4[:, 0:1]], axis=1)
            kd_loop = jnp.concatenate([kd4[:, 0:1] * e4[:, 1:2], kd4[:, 1:2]], axis=1)
            qd_ref[rs, :] = qd_loop.reshape(seg, d).astype(qd_ref.dtype)
            kd_ref[rs, :] = kd_loop.reshape(seg, d).astype(kd_ref.dtype)
            e_ref[s * nloop:(s + 1) * nloop] = e4[:, 0] * e4[:, 1]
            qdb = qd3.reshape(seg, d).astype(BF16)
            kdb = kd3.reshape(seg, d).astype(BF16)
        else:
            qd_ref[rs, :] = qd3.reshape(seg, d).astype(qd_ref.dtype)
            kd_ref[rs, :] = kd3.reshape(seg, d).astype(kd_ref.dtype)
            e_ref[s * nloop:(s + 1) * nloop] = e3

        for h in range(HG_HEADS):
            cs = slice(h * hd, (h + 1) * hd)
            att = lax.dot_general(qh[:, cs], kh[:, cs], NT_DIMS, preferred_element_type=F32)
            att = jnp.where(causal, att, 0.0)
            if pair:
                att1 = lax.dot_general(qdb[:, cs], kdb[:, cs], NT_DIMS, preferred_element_type=F32)
                att = att + jnp.where(cross, att1, 0.0)
            oi_ref[rs, cs] = _dot(att.astype(BF16), vb[:, cs])

    def state_recurrence(s):
        for j in range(nloop):
            r0 = s * seg + j * cl
            for h in range(HG_HEADS):
                cs = slice(h * hd, (h + 1) * hd)
                qd = qd_ref[pl.ds(r0, cl), cs].astype(BF16)
                kd = kd_ref[pl.ds(r0, cl), cs].astype(BF16)
                vv = v_ref[pl.ds(r0, cl), cs].astype(BF16)
                ej = e_ref[s * nloop + j, :, cs]
                if seq:
                    st = st_ref[h]
                    o_int = lax.dot_general(qd, st.astype(BF16), NT_DIMS, preferred_element_type=F32)
                    kv_t = lax.dot_general(vv, kd, TN_DIMS, preferred_element_type=F32)
                    st_ref[h] = st * ej + kv_t
                else:
                    st = s0_ref[j, h]
                    o_int = _dot(qd, st.astype(BF16))
                    e_col = lax.dot_general(_bf16_split_rows(ej), ones_rows, TN_DIMS,
                                            preferred_element_type=F32)
                    kv = lax.dot_general(kd, vv, TN_DIMS, preferred_element_type=F32)
                    so_ref[j, h] = st * e_col + kv
                oi_ref[pl.ds(r0, cl), cs] = oi_ref[pl.ds(r0, cl), cs] + o_int

    def output(s, x, proj):
        g = proj[:, 3 * d:]
        o = oi_ref[s * seg:(s + 1) * seg, :]
        outs = []
        for h in range(HG_HEADS):
            oh = o[:, h * hd:(h + 1) * hd]
            outs.append(oh * lax.rsqrt(jnp.mean(oh * oh, axis=-1, keepdims=True) + RMS_EPS))
        y = jnp.concatenate(outs, axis=1) * gn_ref[...] * (g * _sigmoid(g))
        mix = _dot(y.astype(BF16), wout_ref[...])
        o_ref[s * seg:(s + 1) * seg, :] = _layer_norm(ALPHA * x + mix, lnw_ref[...], lnb_ref[...])

    projected = [project(s) for s in range(nseq)]
    for s in range(nseq):
        gates_and_local_attention(s, projected[s][1])
    for s in range(nseq):
        state_recurrence(s)
    for s in range(nseq):
        output(s, *projected[s])

    if seq:
        @pl.when(ti == pl.num_programs(1) - 1)
        def _():
            for h in range(HG_HEADS):
                so_ref[h] = st_ref[h].T


def _hgrn_kernel_entry(*refs, n_in, n_extra, **kw):
    _hgrn_kernel(*(refs[:n_in] + refs[n_in + n_extra:]), **kw)


def _hgrn_scratch(rows, c, operand_dtype):
    d = D_MODEL
    return [pltpu.VMEM((rows, d), operand_dtype), pltpu.VMEM((rows, d), operand_dtype),
            pltpu.VMEM((rows, d), operand_dtype), pltpu.VMEM((rows // c, 1, d), F32),
            pltpu.VMEM((rows, d), F32)]


def _hgrn_call(x, s0, layer, sprev, win, lb, gn, wout, lnw, lnb, mix_layer, *, tile, nseq=1):
    bsz, t_len, d = x.shape
    hd = HG_HEAD_DIM
    n_layers = win.shape[0]
    seq = s0 is None
    weights = [win, lb, gn, wout]
    w_specs = [_layer_block(w.shape, layer) for w in weights] + [
        _layer_block(lnw.shape, mix_layer), _layer_block(lnb.shape, mix_layer)]
    any_spec = pl.BlockSpec(memory_space=pl.ANY)
    st_shape = jax.ShapeDtypeStruct((n_layers, bsz, HG_HEADS, hd, hd), F32)
    if seq:
        seg, c = tile, HG_CHUNK
        rows = nseq * seg
        grid = (bsz, t_len // rows)
        xspec = pl.BlockSpec((None, rows, d), lambda b, t: (b, t, 0))
        sspec = pl.BlockSpec((None, None, HG_HEADS, hd, hd), lambda b, t: (layer, b, 0, 0, 0))
        in_specs, args = [xspec] + w_specs, [x] + weights + [lnw, lnb]
        scratch = [pltpu.VMEM((HG_HEADS, hd, hd), F32)] + _hgrn_scratch(rows, c, BF16)
        sem = ("parallel", "arbitrary")
        x_in, x_shape = x, x.shape
    else:
        rows, c, nseq = tile * t_len, t_len, 1
        seg = rows
        grid = (bsz // tile,)
        xspec = pl.BlockSpec((rows, d), lambda i: (i, 0))
        sspec = pl.BlockSpec((None, tile, HG_HEADS, hd, hd), lambda i: (layer, i, 0, 0, 0))
        x_in, x_shape = x.reshape(bsz * t_len, d), (bsz * t_len, d)
        in_specs, args = [xspec, sspec] + w_specs, [x_in, s0] + weights + [lnw, lnb]
        scratch = _hgrn_scratch(rows, c, F32)
        sem = ("parallel",)
    n_in = len(args)
    aliases = {}
    if sprev is not None:
        aliases = {n_in: 1}
        in_specs, args = in_specs + [any_spec], args + [sprev]
    kern = functools.partial(_hgrn_kernel_entry, n_in=n_in, n_extra=len(aliases), nseq=nseq, seg=seg, c=c,
                             seq=seq)
    xo, so = pl.pallas_call(
        kern,
        grid=grid,
        in_specs=in_specs,
        out_specs=[xspec, sspec],
        out_shape=[jax.ShapeDtypeStruct(x_shape, F32), st_shape],
        scratch_shapes=scratch,
        input_output_aliases=aliases,
        compiler_params=pltpu.CompilerParams(
            dimension_semantics=sem, vmem_limit_bytes=VMEM_LIMIT_LARGE_BYTES if seq else VMEM_LIMIT_BYTES),
        name="hgrn_mixer_prompt" if seq else "hgrn_mixer_sample",
    )(*args)
    return xo.reshape(bsz, t_len, d), so


def _tile(n, target):
    t = min(n, target)
    while n % t:
        t //= 2
    return t


def kernel(x_prompt, x_sample, state_hgrn, state_s5_re, state_s5_im, p_prompt, p_sample, hg_w_in, hg_lower_bounds, hg_gnorm_w, hg_w_out, s5_a_re, s5_a_im, s5_b_re, s5_b_im, s5_c_re, s5_c_im, s5_d, s5_log_step, s5_w_glu, ln_mix_w, ln_mix_b, ffn_w_gate_up, ffn_w_down, ln_ffn_w, ln_ffn_b, ple_w_proj, ple_w_gate, ple_norm_w):
    bp, tp, d = x_prompt.shape
    bs, ts, _ = x_sample.shape
    depth = ln_mix_w.shape[0]
    ple = p_prompt.shape[-1]
    n_s5 = s5_a_re.shape[0]
    row3 = lambda a: a.reshape(a.shape[0], 1, -1)

    lb_soft = jax.nn.softmax(hg_lower_bounds.astype(F32), axis=0)
    lb_all = row3(jnp.cumsum(lb_soft, axis=0) - lb_soft[0])
    gn_all = row3(jnp.tile(hg_gnorm_w, (1, HG_HEADS)))
    hg_win, hg_wout = hg_w_in.astype(BF16), hg_w_out.astype(BF16)
    lamr, lami, bbr, bbi, cc = _s5_params(s5_a_re, s5_a_im, s5_b_re, s5_b_im, s5_c_re, s5_c_im, s5_log_step)
    dsk_all, wglu_all = row3(s5_d), s5_w_glu.astype(BF16)
    lnm_w, lnm_b = row3(ln_mix_w), row3(ln_mix_b)
    dense_w = (ffn_w_gate_up.astype(BF16), ffn_w_down.astype(BF16), row3(ln_ffn_w), row3(ln_ffn_b),
               ple_w_proj.astype(BF16), ple_w_gate.astype(BF16), row3(ple_norm_w))
    pp = p_prompt.reshape(depth, bp * tp, ple)
    ps = p_sample.reshape(depth, bs * ts, ple)
    h0_s = (state_s5_re.reshape(n_s5, bs, S5_COLS), state_s5_im.reshape(n_s5, bs, S5_COLS))

    xp, xs = x_prompt, x_sample
    hg_p = hg_s = s5_p = s5_s = None
    for i in range(depth):
        j = i // 2
        if i % 2 == 0:
            hg_args = (hg_win, lb_all, gn_all, hg_wout, lnm_w, lnm_b, i)
            xp, hg_p = _hgrn_call(xp, None, j, hg_p, *hg_args, tile=_tile(tp, 256), nseq=4)
            xs, hg_s = _hgrn_call(xs, state_hgrn, j, hg_s, *hg_args, tile=_tile(bs, 16))
        else:
            s5_args = (lamr, lami, bbr, bbi, cc, dsk_all, wglu_all, lnm_w, lnm_b, i)
            xp, *s5_p = _s5_call(xp, None, j, s5_p, *s5_args, bt=bp, tc=_tile(tp, 512 // bp), nsub=2)
            xs, *s5_s = _s5_call(xs, h0_s, j, s5_s, *s5_args, bt=_tile(bs, 256 // ts), tc=ts, nsub=1)
        xp = _dense_call(xp.reshape(bp * tp, d), pp, i, *dense_w, tm=_tile(bp * tp, 1024)).reshape(bp, tp, d)
        xs = _dense_call(xs.reshape(bs * ts, d), ps, i, *dense_w, tm=_tile(bs * ts, 512)).reshape(bs, ts, d)
    s5_shape = lambda a, n: a.reshape(n_s5, n, S5_GROUPS, S5_STATE)
    return (xp, xs, hg_p, s5_shape(s5_p[0], bp), s5_shape(s5_p[1], bp),
            hg_s, s5_shape(s5_s[0], bs), s5_shape(s5_s[1], bs))
```

```python
import functools
import math

import jax
import jax.numpy as jnp
from jax import lax
from jax.experimental import pallas as pl
from jax.experimental.pallas import tpu as pltpu

F32 = jnp.float32
BF16 = jnp.bfloat16

D_MODEL = 1024
HG_HEAD_DIM = 128
HG_HEADS = D_MODEL // HG_HEAD_DIM
HG_QK_SCALE = HG_HEAD_DIM ** -0.5
HG_CHUNK = 16
S5_GROUP = 16
S5_GROUPS = D_MODEL // S5_GROUP
S5_STATE = 64
S5_COLS = S5_GROUPS * S5_STATE
DEPTH = 4
ALPHA = (2 * DEPTH) ** 0.25
LN_EPS = 1e-5
RMS_EPS = 1e-6

V7X_VMEM_BYTES = 64 * 1024 * 1024
VMEM_LIMIT_BYTES = V7X_VMEM_BYTES - 8 * 1024 * 1024
MXU_TILE = 256
LANES = 128
SUBLANES = 8
VREG_ELEMS = SUBLANES * LANES

S5_GROUPS_PER_TILE = MXU_TILE // S5_GROUP
S5_KT = S5_GROUPS // S5_GROUPS_PER_TILE
S5_KT_COLS = S5_GROUPS_PER_TILE * S5_STATE

NT_DIMS = (((1,), (1,)), ((), ()))
TN_DIMS = (((0,), (0,)), ((), ()))


def _dot(a, b):
    return jnp.dot(a, b, preferred_element_type=F32)


def _sigmoid(x):
    return 0.5 * jnp.tanh(0.5 * x) + 0.5


def _sigmoid_rel(x):
    return 1.0 / (1.0 + jnp.exp(-x))


def _layer_norm(x, w, b):
    mu = jnp.mean(x, axis=-1, keepdims=True)
    xc = x - mu
    var = jnp.mean(xc * xc, axis=-1, keepdims=True)
    return xc * lax.rsqrt(var + LN_EPS) * w + b


def _rms_norm(x, w):
    return x * lax.rsqrt(jnp.mean(x * x, axis=-1, keepdims=True) + RMS_EPS) * w


def _bf16_split_rows(e):
    hi = e.astype(BF16).astype(F32)
    r1 = e - hi
    mid = r1.astype(BF16).astype(F32)
    lo = (r1 - mid).astype(BF16).astype(F32)
    row = lax.broadcasted_iota(jnp.int32, (SUBLANES, e.shape[1]), 0)
    return jnp.where(row == 0, hi, jnp.where(row == 1, mid, jnp.where(row == 2, lo, 0.0))).astype(BF16)


def _layer_block(shape, layer):
    nd = len(shape)
    return pl.BlockSpec((None,) + tuple(shape[1:]), lambda *_: (layer,) + (0,) * (nd - 1),
                        pipeline_mode=pl.Buffered(1))


def _dense_kernel(x_ref, p_ref, wgu_ref, wd_ref, lnw_ref, lnb_ref, wproj_ref, wgate_ref, nw_ref, o_ref,
                  *, d_ff, ff_chunk):
    x = x_ref[...]
    xb = x.astype(BF16)
    acc = jnp.zeros(x.shape, F32)
    for j in range(d_ff // ff_chunk):
        lo = j * ff_chunk
        g = _dot(xb, wgu_ref[:, lo:lo + ff_chunk])
        u = _dot(xb, wgu_ref[:, d_ff + lo:d_ff + lo + ff_chunk])
        h = (g * _sigmoid(g)) * u
        acc = acc + _dot(h.astype(BF16), wd_ref[lo:lo + ff_chunk, :])
    x2 = _layer_norm(ALPHA * x + acc, lnw_ref[...], lnb_ref[...])
    e = _dot(p_ref[...].astype(BF16), wproj_ref[...]) * _sigmoid(_dot(x2.astype(BF16), wgate_ref[...]))
    o_ref[...] = x2 + _rms_norm(e, nw_ref[...])


def _dense_call(x, p_all, layer, wgu, wd, lnw, lnb, wproj, wgate, nw, *, tm):
    n, d = x.shape
    d_ff = wd.shape[1]
    ple = p_all.shape[-1]
    row = lambda i: (i, 0)
    kern = functools.partial(_dense_kernel, d_ff=d_ff, ff_chunk=MXU_TILE)
    return pl.pallas_call(
        kern,
        grid=(n // tm,),
        in_specs=[pl.BlockSpec((tm, d), row), pl.BlockSpec((None, tm, ple), lambda i: (layer, i, 0)),
                  _layer_block(wgu.shape, layer), _layer_block(wd.shape, layer),
                  _layer_block(lnw.shape, layer), _layer_block(lnb.shape, layer),
                  _layer_block(wproj.shape, layer), _layer_block(wgate.shape, layer),
                  _layer_block(nw.shape, layer)],
        out_specs=pl.BlockSpec((tm, d), row),
        out_shape=jax.ShapeDtypeStruct((n, d), F32),
        compiler_params=pltpu.CompilerParams(dimension_semantics=("parallel",),
                                             vmem_limit_bytes=VMEM_LIMIT_BYTES),
        name="dense_ffn_ple",
    )(x, p_all, wgu, wd, lnw, lnb, wproj, wgate, nw)


def _s5_kernel(*refs, bt, tc, nsub, has_state):
    if has_state:
        x_ref, h0r_ref, h0i_ref = refs[:3]
        refs = refs[3:]
    else:
        x_ref = refs[0]
        refs = refs[1:]
    (lamr_ref, lami_ref, bbr_ref, bbi_ref, cc_ref, dsk_ref, wglu_ref, lnw_ref, lnb_ref,
     o_ref, hro_ref, hio_ref,
     bur_ref, bui_ref, hsr_ref, hsi_ref, bdr_ref, bdi_ref, cdt_ref, slab_ref) = refs
    d = D_MODEL
    n_slab = d // LANES
    ti = pl.program_id(1)

    @pl.when(ti == 0)
    def _():
        if has_state:
            hsr_ref[...] = h0r_ref[...]
            hsi_ref[...] = h0i_ref[...]
        else:
            hsr_ref[...] = jnp.zeros(hsr_ref.shape, F32)
            hsi_ref[...] = jnp.zeros(hsi_ref.shape, F32)
        rep = MXU_TILE // S5_GROUP
        gi = lax.broadcasted_iota(jnp.int32, (MXU_TILE, S5_KT_COLS), 0) // S5_GROUP
        ci = lax.broadcasted_iota(jnp.int32, (MXU_TILE, S5_KT_COLS), 1) // S5_STATE
        in_mask = gi == ci
        out_mask = jnp.concatenate([in_mask, in_mask], axis=1)
        for kt in range(S5_KT):
            bdr_ref[kt] = jnp.where(in_mask, jnp.concatenate([bbr_ref[kt]] * rep, axis=0), 0.0).astype(BF16)
            bdi_ref[kt] = jnp.where(in_mask, jnp.concatenate([bbi_ref[kt]] * rep, axis=0), 0.0).astype(BF16)
            cdt_ref[kt] = jnp.where(out_mask, jnp.concatenate([cc_ref[kt]] * rep, axis=0), 0.0).astype(BF16)

    ts = tc // nsub

    def load_rows(k):
        return jnp.concatenate([slab_ref[k, s] for s in range(n_slab)], axis=1)

    def project_in(k):
        for s in range(n_slab):
            for b in range(bt):
                slab_ref[k, s, pl.ds(b, ts, stride=bt), :] = x_ref[b, k * ts:(k + 1) * ts, s * LANES:(s + 1) * LANES]
        ub = load_rows(k).astype(BF16)
        for kt in range(S5_KT):
            uk = ub[:, kt * MXU_TILE:(kt + 1) * MXU_TILE]
            bur_ref[k, :, kt * S5_KT_COLS:(kt + 1) * S5_KT_COLS] = _dot(uk, bdr_ref[kt])
            bui_ref[k, :, kt * S5_KT_COLS:(kt + 1) * S5_KT_COLS] = _dot(uk, bdi_ref[kt])

    def recurrence(k):
        cols = 4 * VREG_ELEMS // bt
        for s in range(S5_COLS // cols):
            cs = slice(s * cols, (s + 1) * cols)
            lr = jnp.broadcast_to(lamr_ref[:, cs], (bt, cols))
            li = jnp.broadcast_to(lami_ref[:, cs], (bt, cols))
            hr = hsr_ref[:, cs]
            hi = hsi_ref[:, cs]
            for t in range(ts):
                rs = slice(t * bt, (t + 1) * bt)
                nr = lr * hr - li * hi + bur_ref[k, rs, cs]
                ni = lr * hi + li * hr + bui_ref[k, rs, cs]
                bur_ref[k, rs, cs] = nr
                bui_ref[k, rs, cs] = ni
                hr, hi = nr, ni
            hsr_ref[:, cs] = hr
            hsi_ref[:, cs] = hi

    def project_out(k):
        u = load_rows(k)
        ys = []
        for nt in range(S5_KT):
            hcat = jnp.concatenate([bur_ref[k, :, nt * S5_KT_COLS:(nt + 1) * S5_KT_COLS],
                                    bui_ref[k, :, nt * S5_KT_COLS:(nt + 1) * S5_KT_COLS]], axis=1).astype(BF16)
            ys.append(lax.dot_general(hcat, cdt_ref[nt], NT_DIMS, preferred_element_type=F32))
        y = jnp.concatenate(ys, axis=1) + dsk_ref[...] * u
        y = y * (0.5 * (1.0 + jnp.tanh(math.sqrt(2.0 / math.pi) * (y + 0.044715 * (y * y * y)))))
        z = _dot(y.astype(BF16), wglu_ref[...])
        mix = z[:, :d] * _sigmoid(z[:, d:])
        xn = _layer_norm(ALPHA * u + mix, lnw_ref[...], lnb_ref[...])
        for s in range(n_slab):
            slab_ref[k, s] = xn[:, s * LANES:(s + 1) * LANES]
        for b in range(bt):
            for s in range(n_slab):
                o_ref[b, k * ts:(k + 1) * ts, s * LANES:(s + 1) * LANES] = slab_ref[k, s, pl.ds(b, ts, stride=bt), :]

    for k in range(nsub):
        project_in(k)
    for k in range(nsub):
        recurrence(k)
    hro_ref[...] = hsr_ref[...]
    hio_ref[...] = hsi_ref[...]
    for k in range(nsub):
        project_out(k)


def _s5_call(x, h0, layer, hprev, lamr, lami, bbr, bbi, cc, dsk, wglu, lnw, lnb, mix_layer, *, bt, tc, nsub):
    bsz, t_len, d = x.shape
    n_layers = lamr.shape[0]
    has_state = h0 is not None
    rows = bt * tc // nsub
    xspec = pl.BlockSpec((bt, tc, d), lambda b, t: (b, t, 0))
    sspec = pl.BlockSpec((None, bt, S5_COLS), lambda b, t: (layer, b, 0))
    any_spec = pl.BlockSpec(memory_space=pl.ANY)
    in_specs = [xspec] + ([sspec, sspec] if has_state else []) + [
        _layer_block(lamr.shape, layer), _layer_block(lami.shape, layer), _layer_block(bbr.shape, layer),
        _layer_block(bbi.shape, layer), _layer_block(cc.shape, layer), _layer_block(dsk.shape, layer),
        _layer_block(wglu.shape, layer), _layer_block(lnw.shape, mix_layer), _layer_block(lnb.shape, mix_layer)]
    args = [x] + (list(h0) if has_state else []) + [lamr, lami, bbr, bbi, cc, dsk, wglu, lnw, lnb]
    aliases = {}
    if hprev is not None:
        aliases = {len(args): 1, len(args) + 1: 2}
        in_specs += [any_spec, any_spec]
        args += list(hprev)
    kern = functools.partial(_s5_kernel_entry, bt=bt, tc=tc, nsub=nsub, has_state=has_state,
                             n_extra=len(aliases))
    st_shape = jax.ShapeDtypeStruct((n_layers, bsz, S5_COLS), F32)
    return pl.pallas_call(
        kern,
        grid=(bsz // bt, t_len // tc),
        in_specs=in_specs,
        out_specs=[xspec, sspec, sspec],
        out_shape=[jax.ShapeDtypeStruct(x.shape, F32), st_shape, st_shape],
        scratch_shapes=[pltpu.VMEM((nsub, rows, S5_COLS), F32), pltpu.VMEM((nsub, rows, S5_COLS), F32),
                        pltpu.VMEM((bt, S5_COLS), F32), pltpu.VMEM((bt, S5_COLS), F32),
                        pltpu.VMEM((S5_KT, MXU_TILE, S5_KT_COLS), BF16),
                        pltpu.VMEM((S5_KT, MXU_TILE, S5_KT_COLS), BF16),
                        pltpu.VMEM((S5_KT, MXU_TILE, 2 * S5_KT_COLS), BF16),
                        pltpu.VMEM((nsub, d // LANES, rows, LANES), F32)],
        input_output_aliases=aliases,
        compiler_params=pltpu.CompilerParams(dimension_semantics=("parallel", "arbitrary"),
                                             vmem_limit_bytes=VMEM_LIMIT_BYTES),
        name="s5_mixer",
    )(*args)


def _s5_kernel_entry(*refs, has_state, n_extra, **kw):
    n_in = 1 + (2 if has_state else 0) + 9
    _s5_kernel(*(refs[:n_in] + refs[n_in + n_extra:]), has_state=has_state, **kw)


def _s5_params(a_re, a_im, b_re, b_im, c_re, c_im, log_step):
    n_l = a_re.shape[0]
    delta = jnp.exp(log_step)[..., None]
    er = jnp.exp(a_re * delta)
    lr = er * jnp.cos(a_im * delta)
    li = er * jnp.sin(a_im * delta)
    den = a_re * a_re + a_im * a_im
    qr = ((lr - 1.0) * a_re + li * a_im) / den
    qi = (li * a_re - (lr - 1.0) * a_im) / den
    bbr = qr[..., None] * b_re - qi[..., None] * b_im
    bbi = qr[..., None] * b_im + qi[..., None] * b_re

    def compact_in(bb):
        w = bb.reshape(n_l, S5_KT, S5_GROUPS_PER_TILE, S5_STATE, S5_GROUP).transpose(0, 1, 4, 2, 3)
        return w.reshape(n_l, S5_KT, S5_GROUP, S5_KT_COLS)

    def compact_out(cm):
        w = cm.reshape(n_l, S5_KT, S5_GROUPS_PER_TILE, S5_GROUP, S5_STATE).transpose(0, 1, 3, 2, 4)
        return w.reshape(n_l, S5_KT, S5_GROUP, S5_KT_COLS)

    cc = jnp.concatenate([compact_out(c_re), -compact_out(c_im)], axis=-1)
    return (lr.reshape(n_l, 1, S5_COLS), li.reshape(n_l, 1, S5_COLS), compact_in(bbr), compact_in(bbi), cc)


def _hgrn_kernel(*refs, nseq, seg, c, seq):
    if seq:
        (x_ref, win_ref, lb_ref, gn_ref, wout_ref, lnw_ref, lnb_ref,
         o_ref, so_ref, st_ref, qd_ref, kd_ref, v_ref, e_ref, oi_ref) = refs
        s0_ref = None
    else:
        (x_ref, s0_ref, win_ref, lb_ref, gn_ref, wout_ref, lnw_ref, lnb_ref,
         o_ref, so_ref, qd_ref, kd_ref, v_ref, e_ref, oi_ref) = refs
        st_ref = None
    d = D_MODEL
    hd = HG_HEAD_DIM
    nsub = seg // c
    lg = c.bit_length() - 1

    if seq:
        ti = pl.program_id(1)

        @pl.when(ti == 0)
        def _():
            st_ref[...] = jnp.zeros(st_ref.shape, F32)

    ri = lax.broadcasted_iota(jnp.int32, (seg, seg), 0)
    ci = lax.broadcasted_iota(jnp.int32, (seg, seg), 1)
    causal = jnp.logical_and((ri >> lg) == (ci >> lg), ci <= ri)
    tri = jnp.where(causal, 1.0, 0.0).astype(BF16)
    pair = seq
    cl = 2 * c if pair else c
    nloop = seg // cl
    cross = jnp.logical_and(((ri >> lg) & 1) == 1, (ci >> lg) == (ri >> lg) - 1)
    lb = lb_ref[...]
    ones_rows = jnp.ones((SUBLANES, hd), BF16)

    def project(s):
        x = x_ref[s * seg:(s + 1) * seg, :]
        return x, _dot(x.astype(BF16), win_ref[...])

    def gates_and_local_attention(s, proj):
        rs = slice(s * seg, (s + 1) * seg)
        q = proj[:, :d]
        fz = proj[:, d:2 * d]
        v = proj[:, 2 * d:3 * d]
        f = lb + (1.0 - lb) * _sigmoid_rel(fz)
        logf = jnp.log(f)
        kk = 1.0 - f
        qs = q * _sigmoid(q) * HG_QK_SCALE

        l_hi = logf.astype(BF16)
        l_lo = (logf - l_hi.astype(F32)).astype(BF16)
        b = _dot(tri, l_hi) + _dot(tri, l_lo)
        b3 = b.reshape(nsub, c, d)
        bm = b3[:, c // 2 - 1:c // 2, :]
        bl = b3[:, c - 1:c, :]
        d1 = b3 - bm
        qh3 = qs.reshape(nsub, c, d) * jnp.exp(d1)
        kh3 = kk.reshape(nsub, c, d) * jnp.exp(-d1)
        qd3 = qh3 * jnp.exp(bm)
        kd3 = kh3 * jnp.exp(bl - bm)
        e3 = jnp.exp(bl)
        qh = qh3.reshape(seg, d).astype(BF16)
        kh = kh3.reshape(seg, d).astype(BF16)
        vb = v.astype(BF16)
        v_ref[rs, :] = v.astype(v_ref.dtype)
        if pair:
            qd4 = qd3.reshape(nloop, 2, c, d)
            kd4 = kd3.reshape(nloop, 2, c, d)
            e4 = e3.reshape(nloop, 2, 1, d)
            qd_loop = jnp.concatenate([qd4[:, 0:1], qd4[:, 1:2] * e4[:, 0:1]], axis=1)
            kd_loop = jnp.concatenate([kd4[:, 0:1] * e4[:, 1:2], kd4[:, 1:2]], axis=1)
            qd_ref[rs, :] = qd_loop.reshape(seg, d).astype(qd_ref.dtype)
            kd_ref[rs, :] = kd_loop.reshape(seg, d).astype(kd_ref.dtype)
            e_ref[s * nloop:(s + 1) * nloop] = e4[:, 0] * e4[:, 1]
            qdb = qd3.reshape(seg, d).astype(BF16)
            kdb = kd3.reshape(seg, d).astype(BF16)
        else:
            qd_ref[rs, :] = qd3.reshape(seg, d).astype(qd_ref.dtype)
            kd_ref[rs, :] = kd3.reshape(seg, d).astype(kd_ref.dtype)
            e_ref[s * nloop:(s + 1) * nloop] = e3

        for h in range(HG_HEADS):
            cs = slice(h * hd, (h + 1) * hd)
            att = lax.dot_general(qh[:, cs], kh[:, cs], NT_DIMS, preferred_element_type=F32)
            att = jnp.where(causal, att, 0.0)
            if pair:
                att1 = lax.dot_general(qdb[:, cs], kdb[:, cs], NT_DIMS, preferred_element_type=F32)
                att = att + jnp.where(cross, att1, 0.0)
            oi_ref[rs, cs] = _dot(att.astype(BF16), vb[:, cs])

    def state_recurrence(s):
        for j in range(nloop):
            r0 = s * seg + j * cl
            for h in range(HG_HEADS):
                cs = slice(h * hd, (h + 1) * hd)
                qd = qd_ref[pl.ds(r0, cl), cs].astype(BF16)
                kd = kd_ref[pl.ds(r0, cl), cs].astype(BF16)
                vv = v_ref[pl.ds(r0, cl), cs].astype(BF16)
                ej = e_ref[s * nloop + j, :, cs]
                if seq:
                    st = st_ref[h]
                    o_int = lax.dot_general(qd, st.astype(BF16), NT_DIMS, preferred_element_type=F32)
                    kv_t = lax.dot_general(vv, kd, TN_DIMS, preferred_element_type=F32)
                    st_ref[h] = st * ej + kv_t
                else:
                    st = s0_ref[j, h]
                    o_int = _dot(qd, st.astype(BF16))
                    e_col = lax.dot_general(_bf16_split_rows(ej), ones_rows, TN_DIMS,
                                            preferred_element_type=F32)
                    kv = lax.dot_general(kd, vv, TN_DIMS, preferred_element_type=F32)
                    so_ref[j, h] = st * e_col + kv
                oi_ref[pl.ds(r0, cl), cs] = oi_ref[pl.ds(r0, cl), cs] + o_int

    def output(s, x, proj):
        g = proj[:, 3 * d:]
        o = oi_ref[s * seg:(s + 1) * seg, :]
        outs = []
        for h in range(HG_HEADS):
            oh = o[:, h * hd:(h + 1) * hd]
            outs.append(oh * lax.rsqrt(jnp.mean(oh * oh, axis=-1, keepdims=True) + RMS_EPS))
        y = jnp.concatenate(outs, axis=1) * gn_ref[...] * (g * _sigmoid(g))
        mix = _dot(y.astype(BF16), wout_ref[...])
        o_ref[s * seg:(s + 1) * seg, :] = _layer_norm(ALPHA * x + mix, lnw_ref[...], lnb_ref[...])

    projected = [project(s) for s in range(nseq)]
    for s in range(nseq):
        gates_and_local_attention(s, projected[s][1])
    for s in range(nseq):
        state_recurrence(s)
    for s in range(nseq):
        output(s, *projected[s])

    if seq:
        @pl.when(ti == pl.num_programs(1) - 1)
        def _():
            for h in range(HG_HEADS):
                so_ref[h] = st_ref[h].T


def _hgrn_kernel_entry(*refs, n_in, n_extra, **kw):
    _hgrn_kernel(*(refs[:n_in] + refs[n_in + n_extra:]), **kw)


def _hgrn_scratch(rows, c, operand_dtype):
    d = D_MODEL
    return [pltpu.VMEM((rows, d), operand_dtype), pltpu.VMEM((rows, d), operand_dtype),
            pltpu.VMEM((rows, d), operand_dtype), pltpu.VMEM((rows // c, 1, d), F32),
            pltpu.VMEM((rows, d), F32)]


def _hgrn_call(x, s0, layer, sprev, win, lb, gn, wout, lnw, lnb, mix_layer, *, tile, nseq=1):
    bsz, t_len, d = x.shape
    hd = HG_HEAD_DIM
    n_layers = win.shape[0]
    seq = s0 is None
    weights = [win, lb, gn, wout]
    w_specs = [_layer_block(w.shape, layer) for w in weights] + [
        _layer_block(lnw.shape, mix_layer), _layer_block(lnb.shape, mix_layer)]
    any_spec = pl.BlockSpec(memory_space=pl.ANY)
    st_shape = jax.ShapeDtypeStruct((n_layers, bsz, HG_HEADS, hd, hd), F32)
    if seq:
        seg, c = tile, HG_CHUNK
        rows = nseq * seg
        grid = (bsz, t_len // rows)
        xspec = pl.BlockSpec((None, rows, d), lambda b, t: (b, t, 0))
        sspec = pl.BlockSpec((None, None, HG_HEADS, hd, hd), lambda b, t: (layer, b, 0, 0, 0))
        in_specs, args = [xspec] + w_specs, [x] + weights + [lnw, lnb]
        scratch = [pltpu.VMEM((HG_HEADS, hd, hd), F32)] + _hgrn_scratch(rows, c, BF16)
        sem = ("parallel", "arbitrary")
        x_shape = x.shape
    else:
        rows, c, nseq = tile * t_len, t_len, 1
        seg = rows
        grid = (bsz // tile,)
        xspec = pl.BlockSpec((rows, d), lambda i: (i, 0))
        sspec = pl.BlockSpec((None, tile, HG_HEADS, hd, hd), lambda i: (layer, i, 0, 0, 0))
        x_shape = (bsz * t_len, d)
        in_specs, args = [xspec, sspec] + w_specs, [x.reshape(x_shape), s0] + weights + [lnw, lnb]
        scratch = _hgrn_scratch(rows, c, F32)
        sem = ("parallel",)
    n_in = len(args)
    aliases = {}
    if sprev is not None:
        aliases = {n_in: 1}
        in_specs, args = in_specs + [any_spec], args + [sprev]
    kern = functools.partial(_hgrn_kernel_entry, n_in=n_in, n_extra=len(aliases), nseq=nseq, seg=seg, c=c,
                             seq=seq)
    xo, so = pl.pallas_call(
        kern,
        grid=grid,
        in_specs=in_specs,
        out_specs=[xspec, sspec],
        out_shape=[jax.ShapeDtypeStruct(x_shape, F32), st_shape],
        scratch_shapes=scratch,
        input_output_aliases=aliases,
        compiler_params=pltpu.CompilerParams(dimension_semantics=sem, vmem_limit_bytes=VMEM_LIMIT_BYTES),
        name="hgrn_mixer_prompt" if seq else "hgrn_mixer_sample",
    )(*args)
    return xo.reshape(bsz, t_len, d), so


def _tile(n, target):
    t = min(n, target)
    while n % t:
        t //= 2
    return t


def kernel(x_prompt, x_sample, state_hgrn, state_s5_re, state_s5_im, p_prompt, p_sample, hg_w_in, hg_lower_bounds, hg_gnorm_w, hg_w_out, s5_a_re, s5_a_im, s5_b_re, s5_b_im, s5_c_re, s5_c_im, s5_d, s5_log_step, s5_w_glu, ln_mix_w, ln_mix_b, ffn_w_gate_up, ffn_w_down, ln_ffn_w, ln_ffn_b, ple_w_proj, ple_w_gate, ple_norm_w):
    bp, tp, d = x_prompt.shape
    bs, ts, _ = x_sample.shape
    depth = ln_mix_w.shape[0]
    ple = p_prompt.shape[-1]
    n_s5 = s5_a_re.shape[0]
    row3 = lambda a: a.reshape(a.shape[0], 1, -1)

    lb_soft = jax.nn.softmax(hg_lower_bounds.astype(F32), axis=0)
    lb_all = row3(jnp.cumsum(lb_soft, axis=0) - lb_soft[0])
    gn_all = row3(jnp.tile(hg_gnorm_w, (1, HG_HEADS)))
    hg_win, hg_wout = hg_w_in.astype(BF16), hg_w_out.astype(BF16)
    lamr, lami, bbr, bbi, cc = _s5_params(s5_a_re, s5_a_im, s5_b_re, s5_b_im, s5_c_re, s5_c_im, s5_log_step)
    dsk_all, wglu_all = row3(s5_d), s5_w_glu.astype(BF16)
    lnm_w, lnm_b = row3(ln_mix_w), row3(ln_mix_b)
    dense_w = (ffn_w_gate_up.astype(BF16), ffn_w_down.astype(BF16), row3(ln_ffn_w), row3(ln_ffn_b),
               ple_w_proj.astype(BF16), ple_w_gate.astype(BF16), row3(ple_norm_w))
    pp = p_prompt.reshape(depth, bp * tp, ple)
    ps = p_sample.reshape(depth, bs * ts, ple)
    h0_s = (state_s5_re.reshape(n_s5, bs, S5_COLS), state_s5_im.reshape(n_s5, bs, S5_COLS))

    xp, xs = x_prompt, x_sample
    hg_p = hg_s = s5_p = s5_s = None
    for i in range(depth):
        j = i // 2
        if i % 2 == 0:
            hg_args = (hg_win, lb_all, gn_all, hg_wout, lnm_w, lnm_b, i)
            xp, hg_p = _hgrn_call(xp, None, j, hg_p, *hg_args, tile=_tile(tp, 256), nseq=2)
            xs, hg_s = _hgrn_call(xs, state_hgrn, j, hg_s, *hg_args, tile=_tile(bs, 16))
        else:
            s5_args = (lamr, lami, bbr, bbi, cc, dsk_all, wglu_all, lnm_w, lnm_b, i)
            xp, *s5_p = _s5_call(xp, None, j, s5_p, *s5_args, bt=bp, tc=_tile(tp, 512 // bp), nsub=2)
            xs, *s5_s = _s5_call(xs, h0_s, j, s5_s, *s5_args, bt=_tile(bs, 256 // ts), tc=ts, nsub=1)
        xp = _dense_call(xp.reshape(bp * tp, d), pp, i, *dense_w, tm=_tile(bp * tp, 1024)).reshape(bp, tp, d)
        xs = _dense_call(xs.reshape(bs * ts, d), ps, i, *dense_w, tm=_tile(bs * ts, 512)).reshape(bs, ts, d)
    s5_shape = lambda a, n: a.reshape(n_s5, n, S5_GROUPS, S5_STATE)
    return (xp, xs, hg_p, s5_shape(s5_p[0], bp), s5_shape(s5_p[1], bp),
            hg_s, s5_shape(s5_s[0], bs), s5_shape(s5_s[1], bs))
```

```python
import functools
import math

import jax
import jax.numpy as jnp
from jax import lax
from jax.experimental import pallas as pl
from jax.experimental.pallas import tpu as pltpu

F32 = jnp.float32
BF16 = jnp.bfloat16

D_MODEL = 1024
HG_HEAD_DIM = 128
HG_HEADS = D_MODEL // HG_HEAD_DIM
HG_QK_SCALE = HG_HEAD_DIM ** -0.5
HG_CHUNK = 16
S5_GROUP = 16
S5_GROUPS = D_MODEL // S5_GROUP
S5_STATE = 64
S5_COLS = S5_GROUPS * S5_STATE
DEPTH = 4
ALPHA = (2 * DEPTH) ** 0.25
LN_EPS = 1e-5
RMS_EPS = 1e-6

V7X_VMEM_BYTES = 64 * 1024 * 1024
VMEM_LIMIT_BYTES = V7X_VMEM_BYTES - 8 * 1024 * 1024
MXU_TILE = 256
LANES = 128
SUBLANES = 8
VREG_ELEMS = SUBLANES * LANES

S5_GROUPS_PER_TILE = MXU_TILE // S5_GROUP
S5_KT = S5_GROUPS // S5_GROUPS_PER_TILE
S5_KT_COLS = S5_GROUPS_PER_TILE * S5_STATE

NT_DIMS = (((1,), (1,)), ((), ()))
TN_DIMS = (((0,), (0,)), ((), ()))


def _dot(a, b):
    return jnp.dot(a, b, preferred_element_type=F32)


def _sigmoid(x):
    return 0.5 * jnp.tanh(0.5 * x) + 0.5


def _sigmoid_rel(x):
    return 1.0 / (1.0 + jnp.exp(-x))


def _layer_norm(x, w, b):
    mu = jnp.mean(x, axis=-1, keepdims=True)
    xc = x - mu
    var = jnp.mean(xc * xc, axis=-1, keepdims=True)
    return xc * lax.rsqrt(var + LN_EPS) * w + b


def _rms_norm(x, w):
    return x * lax.rsqrt(jnp.mean(x * x, axis=-1, keepdims=True) + RMS_EPS) * w


def _bf16_split_rows(e):
    hi = e.astype(BF16).astype(F32)
    r1 = e - hi
    mid = r1.astype(BF16).astype(F32)
    lo = (r1 - mid).astype(BF16).astype(F32)
    row = lax.broadcasted_iota(jnp.int32, (SUBLANES, e.shape[1]), 0)
    return jnp.where(row == 0, hi, jnp.where(row == 1, mid, jnp.where(row == 2, lo, 0.0))).astype(BF16)


def _layer_block(shape, layer):
    nd = len(shape)
    return pl.BlockSpec((None,) + tuple(shape[1:]), lambda *_: (layer,) + (0,) * (nd - 1),
                        pipeline_mode=pl.Buffered(1))


def _dense_kernel(x_ref, p_ref, wgu_ref, wd_ref, lnw_ref, lnb_ref, wproj_ref, wgate_ref, nw_ref, o_ref,
                  *, d_ff, ff_chunk):
    x = x_ref[...]
    xb = x.astype(BF16)
    hs = []
    for j in range(d_ff // ff_chunk):
        lo = j * ff_chunk
        g = _dot(xb, wgu_ref[:, lo:lo + ff_chunk])
        u = _dot(xb, wgu_ref[:, d_ff + lo:d_ff + lo + ff_chunk])
        hs.append(((g * _sigmoid(g)) * u).astype(BF16))
    acc = _dot(jnp.concatenate(hs, axis=1), wd_ref[...])
    x2 = _layer_norm(ALPHA * x + acc, lnw_ref[...], lnb_ref[...])
    e = _dot(p_ref[...].astype(BF16), wproj_ref[...]) * _sigmoid(_dot(x2.astype(BF16), wgate_ref[...]))
    o_ref[...] = x2 + _rms_norm(e, nw_ref[...])


def _dense_call(x, p_all, layer, wgu, wd, lnw, lnb, wproj, wgate, nw, *, tm):
    n, d = x.shape
    d_ff = wd.shape[1]
    ple = p_all.shape[-1]
    row = lambda i: (i, 0)
    kern = functools.partial(_dense_kernel, d_ff=d_ff, ff_chunk=MXU_TILE)
    return pl.pallas_call(
        kern,
        grid=(n // tm,),
        in_specs=[pl.BlockSpec((tm, d), row), pl.BlockSpec((None, tm, ple), lambda i: (layer, i, 0)),
                  _layer_block(wgu.shape, layer), _layer_block(wd.shape, layer),
                  _layer_block(lnw.shape, layer), _layer_block(lnb.shape, layer),
                  _layer_block(wproj.shape, layer), _layer_block(wgate.shape, layer),
                  _layer_block(nw.shape, layer)],
        out_specs=pl.BlockSpec((tm, d), row),
        out_shape=jax.ShapeDtypeStruct((n, d), F32),
        compiler_params=pltpu.CompilerParams(dimension_semantics=("parallel",),
                                             vmem_limit_bytes=VMEM_LIMIT_BYTES),
        name="dense_ffn_ple",
    )(x, p_all, wgu, wd, lnw, lnb, wproj, wgate, nw)


def _s5_kernel(*refs, bt, tc, nsub, has_state):
    if has_state:
        x_ref, h0r_ref, h0i_ref = refs[:3]
        refs = refs[3:]
    else:
        x_ref = refs[0]
        refs = refs[1:]
    (lamr_ref, lami_ref, bbr_ref, bbi_ref, cc_ref, dsk_ref, wglu_ref, lnw_ref, lnb_ref,
     o_ref, hro_ref, hio_ref,
     bur_ref, bui_ref, hsr_ref, hsi_ref, bdr_ref, bdi_ref, cdt_ref, slab_ref) = refs
    d = D_MODEL
    n_slab = d // LANES
    ti = pl.program_id(1)

    @pl.when(ti == 0)
    def _():
        if has_state:
            hsr_ref[...] = h0r_ref[...]
            hsi_ref[...] = h0i_ref[...]
        else:
            hsr_ref[...] = jnp.zeros(hsr_ref.shape, F32)
            hsi_ref[...] = jnp.zeros(hsi_ref.shape, F32)
        rep = MXU_TILE // S5_GROUP
        gi = lax.broadcasted_iota(jnp.int32, (MXU_TILE, S5_KT_COLS), 0) // S5_GROUP
        ci = lax.broadcasted_iota(jnp.int32, (MXU_TILE, S5_KT_COLS), 1) // S5_STATE
        in_mask = gi == ci
        out_mask = jnp.concatenate([in_mask, in_mask], axis=1)
        for kt in range(S5_KT):
            bdr_ref[kt] = jnp.where(in_mask, jnp.concatenate([bbr_ref[kt]] * rep, axis=0), 0.0).astype(BF16)
            bdi_ref[kt] = jnp.where(in_mask, jnp.concatenate([bbi_ref[kt]] * rep, axis=0), 0.0).astype(BF16)
            cdt_ref[kt] = jnp.where(out_mask, jnp.concatenate([cc_ref[kt]] * rep, axis=0), 0.0).astype(BF16)

    ts = tc // nsub

    def load_rows(k):
        return jnp.concatenate([slab_ref[k, s] for s in range(n_slab)], axis=1)

    def project_in(k):
        for s in range(n_slab):
            for b in range(bt):
                slab_ref[k, s, pl.ds(b, ts, stride=bt), :] = x_ref[b, k * ts:(k + 1) * ts, s * LANES:(s + 1) * LANES]
        ub = load_rows(k).astype(BF16)
        for kt in range(S5_KT):
            uk = ub[:, kt * MXU_TILE:(kt + 1) * MXU_TILE]
            bur_ref[k, :, kt * S5_KT_COLS:(kt + 1) * S5_KT_COLS] = _dot(uk, bdr_ref[kt])
            bui_ref[k, :, kt * S5_KT_COLS:(kt + 1) * S5_KT_COLS] = _dot(uk, bdi_ref[kt])

    def recurrence(k):
        cols = 4 * VREG_ELEMS // bt
        for s in range(S5_COLS // cols):
            cs = slice(s * cols, (s + 1) * cols)
            lr = jnp.broadcast_to(lamr_ref[:, cs], (bt, cols))
            li = jnp.broadcast_to(lami_ref[:, cs], (bt, cols))
            hr = hsr_ref[:, cs]
            hi = hsi_ref[:, cs]
            for t in range(ts):
                rs = slice(t * bt, (t + 1) * bt)
                nr = lr * hr - li * hi + bur_ref[k, rs, cs]
                ni = lr * hi + li * hr + bui_ref[k, rs, cs]
                bur_ref[k, rs, cs] = nr
                bui_ref[k, rs, cs] = ni
                hr, hi = nr, ni
            hsr_ref[:, cs] = hr
            hsi_ref[:, cs] = hi

    def project_out(k):
        u = load_rows(k)
        ys = []
        for nt in range(S5_KT):
            hcat = jnp.concatenate([bur_ref[k, :, nt * S5_KT_COLS:(nt + 1) * S5_KT_COLS],
                                    bui_ref[k, :, nt * S5_KT_COLS:(nt + 1) * S5_KT_COLS]], axis=1).astype(BF16)
            ys.append(lax.dot_general(hcat, cdt_ref[nt], NT_DIMS, preferred_element_type=F32))
        y = jnp.concatenate(ys, axis=1) + dsk_ref[...] * u
        y = y * (0.5 * (1.0 + jnp.tanh(math.sqrt(2.0 / math.pi) * (y + 0.044715 * (y * y * y)))))
        z = _dot(y.astype(BF16), wglu_ref[...])
        mix = z[:, :d] * _sigmoid(z[:, d:])
        xn = _layer_norm(ALPHA * u + mix, lnw_ref[...], lnb_ref[...])
        for s in range(n_slab):
            slab_ref[k, s] = xn[:, s * LANES:(s + 1) * LANES]
        for b in range(bt):
            for s in range(n_slab):
                o_ref[b, k * ts:(k + 1) * ts, s * LANES:(s + 1) * LANES] = slab_ref[k, s, pl.ds(b, ts, stride=bt), :]

    for k in range(nsub):
        project_in(k)
    for k in range(nsub):
        recurrence(k)
    hro_ref[...] = hsr_ref[...]
    hio_ref[...] = hsi_ref[...]
    for k in range(nsub):
        project_out(k)


def _s5_call(x, h0, layer, hprev, lamr, lami, bbr, bbi, cc, dsk, wglu, lnw, lnb, mix_layer, *, bt, tc, nsub):
    bsz, t_len, d = x.shape
    n_layers = lamr.shape[0]
    has_state = h0 is not None
    rows = bt * tc // nsub
    xspec = pl.BlockSpec((bt, tc, d), lambda b, t: (b, t, 0))
    sspec = pl.BlockSpec((None, bt, S5_COLS), lambda b, t: (layer, b, 0))
    any_spec = pl.BlockSpec(memory_space=pl.ANY)
    in_specs = [xspec] + ([sspec, sspec] if has_state else []) + [
        _layer_block(lamr.shape, layer), _layer_block(lami.shape, layer), _layer_block(bbr.shape, layer),
        _layer_block(bbi.shape, layer), _layer_block(cc.shape, layer), _layer_block(dsk.shape, layer),
        _layer_block(wglu.shape, layer), _layer_block(lnw.shape, mix_layer), _layer_block(lnb.shape, mix_layer)]
    args = [x] + (list(h0) if has_state else []) + [lamr, lami, bbr, bbi, cc, dsk, wglu, lnw, lnb]
    aliases = {}
    if hprev is not None:
        aliases = {len(args): 1, len(args) + 1: 2}
        in_specs += [any_spec, any_spec]
        args += list(hprev)
    kern = functools.partial(_s5_kernel_entry, bt=bt, tc=tc, nsub=nsub, has_state=has_state,
                             n_extra=len(aliases))
    st_shape = jax.ShapeDtypeStruct((n_layers, bsz, S5_COLS), F32)
    return pl.pallas_call(
        kern,
        grid=(bsz // bt, t_len // tc),
        in_specs=in_specs,
        out_specs=[xspec, sspec, sspec],
        out_shape=[jax.ShapeDtypeStruct(x.shape, F32), st_shape, st_shape],
        scratch_shapes=[pltpu.VMEM((nsub, rows, S5_COLS), F32), pltpu.VMEM((nsub, rows, S5_COLS), F32),
                        pltpu.VMEM((bt, S5_COLS), F32), pltpu.VMEM((bt, S5_COLS), F32),
                        pltpu.VMEM((S5_KT, MXU_TILE, S5_KT_COLS), BF16),
                        pltpu.VMEM((S5_KT, MXU_TILE, S5_KT_COLS), BF16),
                        pltpu.VMEM((S5_KT, MXU_TILE, 2 * S5_KT_COLS), BF16),
                        pltpu.VMEM((nsub, d // LANES, rows, LANES), F32)],
        input_output_aliases=aliases,
        compiler_params=pltpu.CompilerParams(dimension_semantics=("parallel", "arbitrary"),
                                             vmem_limit_bytes=VMEM_LIMIT_BYTES),
        name="s5_mixer",
    )(*args)


def _s5_kernel_entry(*refs, has_state, n_extra, **kw):
    n_in = 1 + (2 if has_state else 0) + 9
    _s5_kernel(*(refs[:n_in] + refs[n_in + n_extra:]), has_state=has_state, **kw)


def _s5_params(a_re, a_im, b_re, b_im, c_re, c_im, log_step):
    n_l = a_re.shape[0]
    delta = jnp.exp(log_step)[..., None]
    er = jnp.exp(a_re * delta)
    lr = er * jnp.cos(a_im * delta)
    li = er * jnp.sin(a_im * delta)
    den = a_re * a_re + a_im * a_im
    qr = ((lr - 1.0) * a_re + li * a_im) / den
    qi = (li * a_re - (lr - 1.0) * a_im) / den
    bbr = qr[..., None] * b_re - qi[..., None] * b_im
    bbi = qr[..., None] * b_im + qi[..., None] * b_re

    def compact_in(bb):
        w = bb.reshape(n_l, S5_KT, S5_GROUPS_PER_TILE, S5_STATE, S5_GROUP).transpose(0, 1, 4, 2, 3)
        return w.reshape(n_l, S5_KT, S5_GROUP, S5_KT_COLS)

    def compact_out(cm):
        w = cm.reshape(n_l, S5_KT, S5_GROUPS_PER_TILE, S5_GROUP, S5_STATE).transpose(0, 1, 3, 2, 4)
        return w.reshape(n_l, S5_KT, S5_GROUP, S5_KT_COLS)

    cc = jnp.concatenate([compact_out(c_re), -compact_out(c_im)], axis=-1)
    return (lr.reshape(n_l, 1, S5_COLS), li.reshape(n_l, 1, S5_COLS), compact_in(bbr), compact_in(bbi), cc)


def _hgrn_kernel(*refs, nseq, seg, c, seq):
    if seq:
        (x_ref, win_ref, lb_ref, gn_ref, wout_ref, lnw_ref, lnb_ref,
         o_ref, so_ref, st_ref, qd_ref, kd_ref, v_ref, e_ref, oi_ref) = refs
        s0_ref = None
    else:
        (x_ref, s0_ref, win_ref, lb_ref, gn_ref, wout_ref, lnw_ref, lnb_ref,
         o_ref, so_ref, qd_ref, kd_ref, v_ref, e_ref, oi_ref) = refs
        st_ref = None
    d = D_MODEL
    hd = HG_HEAD_DIM
    nsub = seg // c
    lg = c.bit_length() - 1

    if seq:
        ti = pl.program_id(1)

        @pl.when(ti == 0)
        def _():
            st_ref[...] = jnp.zeros(st_ref.shape, F32)

    ri = lax.broadcasted_iota(jnp.int32, (seg, seg), 0)
    ci = lax.broadcasted_iota(jnp.int32, (seg, seg), 1)
    causal = jnp.logical_and((ri >> lg) == (ci >> lg), ci <= ri)
    tri = jnp.where(causal, 1.0, 0.0).astype(BF16)
    pair = seq
    cl = 2 * c if pair else c
    nloop = seg // cl
    cross = jnp.logical_and(((ri >> lg) & 1) == 1, (ci >> lg) == (ri >> lg) - 1)
    lb = lb_ref[...]
    ones_rows = jnp.ones((SUBLANES, hd), BF16)

    def project(s):
        x = x_ref[s * seg:(s + 1) * seg, :]
        return x, _dot(x.astype(BF16), win_ref[...])

    def gates_and_local_attention(s, proj):
        rs = slice(s * seg, (s + 1) * seg)
        q = proj[:, :d]
        fz = proj[:, d:2 * d]
        v = proj[:, 2 * d:3 * d]
        f = lb + (1.0 - lb) * _sigmoid_rel(fz)
        logf = jnp.log(f)
        kk = 1.0 - f
        qs = q * _sigmoid(q) * HG_QK_SCALE

        l_hi = logf.astype(BF16)
        l_lo = (logf - l_hi.astype(F32)).astype(BF16)
        b = _dot(tri, l_hi) + _dot(tri, l_lo)
        b3 = b.reshape(nsub, c, d)
        bm = b3[:, c // 2 - 1:c // 2, :]
        bl = b3[:, c - 1:c, :]
        d1 = b3 - bm
        qh3 = qs.reshape(nsub, c, d) * jnp.exp(d1)
        kh3 = kk.reshape(nsub, c, d) * jnp.exp(-d1)
        qd3 = qh3 * jnp.exp(bm)
        kd3 = kh3 * jnp.exp(bl - bm)
        e3 = jnp.exp(bl)
        qh = qh3.reshape(seg, d).astype(BF16)
        kh = kh3.reshape(seg, d).astype(BF16)
        vb = v.astype(BF16)
        v_ref[rs, :] = v.astype(v_ref.dtype)
        if pair:
            qd4 = qd3.reshape(nloop, 2, c, d)
            kd4 = kd3.reshape(nloop, 2, c, d)
            e4 = e3.reshape(nloop, 2, 1, d)
            qd_loop = jnp.concatenate([qd4[:, 0:1], qd4[:, 1:2] * e4[:, 0:1]], axis=1)
            kd_loop = jnp.concatenate([kd4[:, 0:1] * e4[:, 1:2], kd4[:, 1:2]], axis=1)
            qd_ref[rs, :] = qd_loop.reshape(seg, d).astype(qd_ref.dtype)
            kd_ref[rs, :] = kd_loop.reshape(seg, d).astype(kd_ref.dtype)
            e_ref[s * nloop:(s + 1) * nloop] = e4[:, 0] * e4[:, 1]
            qdb = qd3.reshape(seg, d).astype(BF16)
            kdb = kd3.reshape(seg, d).astype(BF16)
        else:
            qd_ref[rs, :] = qd3.reshape(seg, d).astype(qd_ref.dtype)
            kd_ref[rs, :] = kd3.reshape(seg, d).astype(kd_ref.dtype)
            e_ref[s * nloop:(s + 1) * nloop] = e3

        for h in range(HG_HEADS):
            cs = slice(h * hd, (h + 1) * hd)
            att = lax.dot_general(qh[:, cs], kh[:, cs], NT_DIMS, preferred_element_type=F32)
            att = jnp.where(causal, att, 0.0)
            if pair:
                att1 = lax.dot_general(qdb[:, cs], kdb[:, cs], NT_DIMS, preferred_element_type=F32)
                att = att + jnp.where(cross, att1, 0.0)
            oi_ref[rs, cs] = _dot(att.astype(BF16), vb[:, cs])

    def state_recurrence(s):
        for j in range(nloop):
            r0 = s * seg + j * cl
            for h in range(HG_HEADS):
                cs = slice(h * hd, (h + 1) * hd)
                qd = qd_ref[pl.ds(r0, cl), cs].astype(BF16)
                kd = kd_ref[pl.ds(r0, cl), cs].astype(BF16)
                vv = v_ref[pl.ds(r0, cl), cs].astype(BF16)
                ej = e_ref[s * nloop + j, :, cs]
                if seq:
                    st = st_ref[h]
                    o_int = lax.dot_general(qd, st.astype(BF16), NT_DIMS, preferred_element_type=F32)
                    kv_t = lax.dot_general(vv, kd, TN_DIMS, preferred_element_type=F32)
                    st_ref[h] = st * ej + kv_t
                else:
                    st = s0_ref[j, h]
                    o_int = _dot(qd, st.astype(BF16))
                    e_col = lax.dot_general(_bf16_split_rows(ej), ones_rows, TN_DIMS,
                                            preferred_element_type=F32)
                    kv = lax.dot_general(kd, vv, TN_DIMS, preferred_element_type=F32)
                    so_ref[j, h] = st * e_col + kv
                oi_ref[pl.ds(r0, cl), cs] = oi_ref[pl.ds(r0, cl), cs] + o_int

    def output(s, x, proj):
        g = proj[:, 3 * d:]
        o = oi_ref[s * seg:(s + 1) * seg, :]
        outs = []
        for h in range(HG_HEADS):
            oh = o[:, h * hd:(h + 1) * hd]
            outs.append(oh * lax.rsqrt(jnp.mean(oh * oh, axis=-1, keepdims=True) + RMS_EPS))
        y = jnp.concatenate(outs, axis=1) * gn_ref[...] * (g * _sigmoid(g))
        mix = _dot(y.astype(BF16), wout_ref[...])
        o_ref[s * seg:(s + 1) * seg, :] = _layer_norm(ALPHA * x + mix, lnw_ref[...], lnb_ref[...])

    projected = [project(s) for s in range(nseq)]
    for s in range(nseq):
        gates_and_local_attention(s, projected[s][1])
    for s in range(nseq):
        state_recurrence(s)
    for s in range(nseq):
        output(s, *projected[s])

    if seq:
        @pl.when(ti == pl.num_programs(1) - 1)
        def _():
            for h in range(HG_HEADS):
                so_ref[h] = st_ref[h].T


def _hgrn_kernel_entry(*refs, n_in, n_extra, **kw):
    _hgrn_kernel(*(refs[:n_in] + refs[n_in + n_extra:]), **kw)


def _hgrn_scratch(rows, c, operand_dtype):
    d = D_MODEL
    return [pltpu.VMEM((rows, d), operand_dtype), pltpu.VMEM((rows, d), operand_dtype),
            pltpu.VMEM((rows, d), operand_dtype), pltpu.VMEM((rows // c, 1, d), F32),
            pltpu.VMEM((rows, d), F32)]


def _hgrn_call(x, s0, layer, sprev, win, lb, gn, wout, lnw, lnb, mix_layer, *, tile, nseq=1):
    bsz, t_len, d = x.shape
    hd = HG_HEAD_DIM
    n_layers = win.shape[0]
    seq = s0 is None
    weights = [win, lb, gn, wout]
    w_specs = [_layer_block(w.shape, layer) for w in weights] + [
        _layer_block(lnw.shape, mix_layer), _layer_block(lnb.shape, mix_layer)]
    any_spec = pl.BlockSpec(memory_space=pl.ANY)
    st_shape = jax.ShapeDtypeStruct((n_layers, bsz, HG_HEADS, hd, hd), F32)
    if seq:
        seg, c = tile, HG_CHUNK
        rows = nseq * seg
        grid = (bsz, t_len // rows)
        xspec = pl.BlockSpec((None, rows, d), lambda b, t: (b, t, 0))
        sspec = pl.BlockSpec((None, None, HG_HEADS, hd, hd), lambda b, t: (layer, b, 0, 0, 0))
        in_specs, args = [xspec] + w_specs, [x] + weights + [lnw, lnb]
        scratch = [pltpu.VMEM((HG_HEADS, hd, hd), F32)] + _hgrn_scratch(rows, c, BF16)
        sem = ("parallel", "arbitrary")
        x_shape = x.shape
    else:
        rows, c, nseq = tile * t_len, t_len, 1
        seg = rows
        grid = (bsz // tile,)
        xspec = pl.BlockSpec((rows, d), lambda i: (i, 0))
        sspec = pl.BlockSpec((None, tile, HG_HEADS, hd, hd), lambda i: (layer, i, 0, 0, 0))
        x_shape = (bsz * t_len, d)
        in_specs, args = [xspec, sspec] + w_specs, [x.reshape(x_shape), s0] + weights + [lnw, lnb]
        scratch = _hgrn_scratch(rows, c, F32)
        sem = ("parallel",)
    n_in = len(args)
    aliases = {}
    if sprev is not None:
        aliases = {n_in: 1}
        in_specs, args = in_specs + [any_spec], args + [sprev]
    kern = functools.partial(_hgrn_kernel_entry, n_in=n_in, n_extra=len(aliases), nseq=nseq, seg=seg, c=c,
                             seq=seq)
    xo, so = pl.pallas_call(
        kern,
        grid=grid,
        in_specs=in_specs,
        out_specs=[xspec, sspec],
        out_shape=[jax.ShapeDtypeStruct(x_shape, F32), st_shape],
        scratch_shapes=scratch,
        input_output_aliases=aliases,
        compiler_params=pltpu.CompilerParams(dimension_semantics=sem, vmem_limit_bytes=VMEM_LIMIT_BYTES),
        name="hgrn_mixer_prompt" if seq else "hgrn_mixer_sample",
    )(*args)
    return xo.reshape(bsz, t_len, d), so


def _tile(n, target):
    t = min(n, target)
    while n % t:
        t //= 2
    return t


def kernel(x_prompt, x_sample, state_hgrn, state_s5_re, state_s5_im, p_prompt, p_sample, hg_w_in, hg_lower_bounds, hg_gnorm_w, hg_w_out, s5_a_re, s5_a_im, s5_b_re, s5_b_im, s5_c_re, s5_c_im, s5_d, s5_log_step, s5_w_glu, ln_mix_w, ln_mix_b, ffn_w_gate_up, ffn_w_down, ln_ffn_w, ln_ffn_b, ple_w_proj, ple_w_gate, ple_norm_w):
    bp, tp, d = x_prompt.shape
    bs, ts, _ = x_sample.shape
    depth = ln_mix_w.shape[0]
    ple = p_prompt.shape[-1]
    n_s5 = s5_a_re.shape[0]
    row3 = lambda a: a.reshape(a.shape[0], 1, -1)

    lb_soft = jax.nn.softmax(hg_lower_bounds.astype(F32), axis=0)
    lb_all = row3(jnp.cumsum(lb_soft, axis=0) - lb_soft[0])
    gn_all = row3(jnp.tile(hg_gnorm_w, (1, HG_HEADS)))
    hg_win, hg_wout = hg_w_in.astype(BF16), hg_w_out.astype(BF16)
    lamr, lami, bbr, bbi, cc = _s5_params(s5_a_re, s5_a_im, s5_b_re, s5_b_im, s5_c_re, s5_c_im, s5_log_step)
    dsk_all, wglu_all = row3(s5_d), s5_w_glu.astype(BF16)
    lnm_w, lnm_b = row3(ln_mix_w), row3(ln_mix_b)
    dense_w = (ffn_w_gate_up.astype(BF16), ffn_w_down.astype(BF16), row3(ln_ffn_w), row3(ln_ffn_b),
               ple_w_proj.astype(BF16), ple_w_gate.astype(BF16), row3(ple_norm_w))
    pp = p_prompt.reshape(depth, bp * tp, ple)
    ps = p_sample.reshape(depth, bs * ts, ple)
    h0_s = (state_s5_re.reshape(n_s5, bs, S5_COLS), state_s5_im.reshape(n_s5, bs, S5_COLS))

    xp, xs = x_prompt, x_sample
    hg_p = hg_s = s5_p = s5_s = None
    for i in range(depth):
        j = i // 2
        if i % 2 == 0:
            hg_args = (hg_win, lb_all, gn_all, hg_wout, lnm_w, lnm_b, i)
            xp, hg_p = _hgrn_call(xp, None, j, hg_p, *hg_args, tile=_tile(tp, 256), nseq=2)
            xs, hg_s = _hgrn_call(xs, state_hgrn, j, hg_s, *hg_args, tile=_tile(bs, 16))
        else:
            s5_args = (lamr, lami, bbr, bbi, cc, dsk_all, wglu_all, lnm_w, lnm_b, i)
            xp, *s5_p = _s5_call(xp, None, j, s5_p, *s5_args, bt=bp, tc=_tile(tp, 512 // bp), nsub=2)
            xs, *s5_s = _s5_call(xs, h0_s, j, s5_s, *s5_args, bt=_tile(bs, 256 // ts), tc=ts, nsub=1)
        xp = _dense_call(xp.reshape(bp * tp, d), pp, i, *dense_w, tm=_tile(bp * tp, 1024)).reshape(bp, tp, d)
        xs = _dense_call(xs.reshape(bs * ts, d), ps, i, *dense_w, tm=_tile(bs * ts, 512)).reshape(bs, ts, d)
    s5_shape = lambda a, n: a.reshape(n_s5, n, S5_GROUPS, S5_STATE)
    return (xp, xs, hg_p, s5_shape(s5_p[0], bp), s5_shape(s5_p[1], bp),
            hg_s, s5_shape(s5_s[0], bs), s5_shape(s5_s[1], bs))
```
